```python
import jax, jax.numpy as jnp
from jax import lax
import numpy as np

D_MODEL = 1024
BATCH = 2
SEQ = 8192
DEPTH = 2

MLA_HEADS = 4
QK_NOPE = 128
QK_ROPE = 64
V_HEAD = 128
Q_RANK = 384
KV_RANK = 256
ROPE_THETA = 10000.0
Q_BLOCK = 128
HG_HEADS = 4
HG_KEY = 128
HG_VAL = 128
HG_CHUNK = 64
MLA_WIDTH = MLA_HEADS * V_HEAD
HG_WIDTH = HG_HEADS * HG_VAL
HG_FDIM = HG_HEADS * HG_KEY
MIX_WIDTH = MLA_WIDTH + HG_WIDTH
IN_COLS = Q_RANK + KV_RANK + QK_ROPE + 2 * HG_FDIM + 2 * HG_WIDTH
DENSE_FF = 2816
N_EXPERTS = 8
TOP_K = 2
EXPERT_FF = 3584
N_DENSE = (DEPTH + 1) // 2
N_MOE = DEPTH // 2
EPS = 1e-6
F_FLOOR = 1e-30

kernel_name = "hybrid_mla_hgrn2_moe_block"

F32 = jnp.float32
NEG_BIG = float(np.finfo(np.float32).min)


def _rmsnorm(x, gain):
    xf = x.astype(F32)
    y = xf * lax.rsqrt(jnp.mean(xf * xf, axis=-1, keepdims=True) + EPS)
    return (y * gain.astype(F32)).astype(x.dtype)


def _rope_tables(positions):
    inv = ROPE_THETA ** (-jnp.arange(0, QK_ROPE, 2, dtype=F32) / QK_ROPE)
    ang = positions.astype(F32)[..., None] * inv
    return jnp.cos(ang), jnp.sin(ang)


def _apply_rope(x, cos, sin):
    x1, x2 = jnp.split(x, 2, axis=-1)
    cos = cos.astype(x.dtype)
    sin = sin.astype(x.dtype)
    return jnp.concatenate([x1 * cos - x2 * sin, x1 * sin + x2 * cos], axis=-1)


def _mla(c_q, c_kv, k_pe, positions, q_a_norm, w_q_b, kv_a_norm, w_kv_b):
    B, S, _ = c_q.shape
    q = (_rmsnorm(c_q, q_a_norm) @ w_q_b).reshape(B, S, MLA_HEADS, QK_NOPE + QK_ROPE)
    q_nope, q_pe = q[..., :QK_NOPE], q[..., QK_NOPE:]
    cos, sin = _rope_tables(positions)
    q_pe = _apply_rope(q_pe, cos[:, :, None, :], sin[:, :, None, :])
    k_pe = _apply_rope(k_pe, cos, sin)
    kv = (_rmsnorm(c_kv, kv_a_norm) @ w_kv_b).reshape(B, S, MLA_HEADS, QK_NOPE + V_HEAD)
    k_nope, v = kv[..., :QK_NOPE], kv[..., QK_NOPE:]
    scale = (QK_NOPE + QK_ROPE) ** -0.5
    nb = S // Q_BLOCK
    qn_blk = (q_nope * scale).reshape(B, nb, Q_BLOCK, MLA_HEADS, QK_NOPE).transpose(1, 0, 2, 3, 4)
    qp_blk = (q_pe * scale).reshape(B, nb, Q_BLOCK, MLA_HEADS, QK_ROPE).transpose(1, 0, 2, 3, 4)
    starts = jnp.arange(nb, dtype=jnp.int32) * Q_BLOCK
    key_idx = jnp.arange(S, dtype=jnp.int32)

    def block(args):
        qn, qp, start = args
        s = jnp.einsum('bqhd,bkhd->bhqk', qn, k_nope, preferred_element_type=F32)
        s = s + jnp.einsum('bqhr,bkr->bhqk', qp, k_pe, preferred_element_type=F32)
        q_idx = start + jnp.arange(Q_BLOCK, dtype=jnp.int32)
        s = jnp.where(key_idx[None, :] <= q_idx[:, None], s, NEG_BIG)
        p = jax.nn.softmax(s, axis=-1).astype(v.dtype)
        return jnp.einsum('bhqk,bkhd->bqhd', p, v)

    o = lax.map(block, (qn_blk, qp_blk, starts))
    return o.transpose(1, 0, 2, 3, 4).reshape(B, S, MLA_WIDTH)


def _hgrn2(q_raw, f_raw, i_in, g_raw, lower_bound, out_norm):
    B, S, _ = q_raw.shape
    nc = S // HG_CHUNK
    q = jax.nn.silu(q_raw.astype(F32))
    z = f_raw.astype(F32)
    lb = lower_bound.astype(F32)
    f = lb + (1.0 - lb) * jax.nn.sigmoid(z)
    log_f = jnp.log(jnp.maximum(f, F_FLOOR))
    k = (1.0 - lb) * jax.nn.sigmoid(-z)

    def to_chunks(t, d):
        return t.reshape(B, nc, HG_CHUNK, HG_HEADS, d).transpose(1, 0, 3, 2, 4)

    qc = to_chunks(q, HG_KEY)
    kc = to_chunks(k, HG_KEY)
    lfc = to_chunks(log_f, HG_KEY)
    vc = to_chunks(i_in.astype(F32), HG_VAL)
    tri = jnp.tril(jnp.ones((HG_CHUNK, HG_CHUNK), dtype=bool))[:, :, None]

    def step(state, inp):
        qb, kb, vb, lfb = inp
        b = jnp.cumsum(lfb, axis=2)
        diff = b[:, :, :, None, :] - b[:, :, None, :, :]
        decay = jnp.where(tri, jnp.exp(jnp.where(tri, diff, 0.0)), 0.0)
        attn = jnp.einsum('bhtd,bhsd,bhtsd->bhts', qb, kb, decay)
        o = jnp.einsum('bhts,bhsv->bhtv', attn, vb) + jnp.einsum('bhtd,bhdv->bhtv', qb * jnp.exp(b), state)
        b_last = b[:, :, -1:, :]
        new_state = jnp.exp(b_last[:, :, 0, :])[..., None] * state + jnp.einsum('bhsd,bhsv->bhdv', kb * jnp.exp(b_last - b), vb)
        return new_state, o

    s0 = jnp.zeros((B, HG_HEADS, HG_KEY, HG_VAL), F32)
    _, o = lax.scan(step, s0, (qc, kc, vc, lfc))
    o = o.transpose(1, 0, 3, 2, 4).reshape(B, S, HG_HEADS, HG_VAL)
    o = _rmsnorm(o, out_norm) * jax.nn.silu(g_raw.astype(F32).reshape(B, S, HG_HEADS, HG_VAL))
    return o.reshape(B, S, HG_WIDTH)


def _swiglu(h, w_gate, w_up, w_down):
    return (jax.nn.silu(h @ w_gate) * (h @ w_up)) @ w_down


def _moe(h, w_router, w_gate, w_up, w_down):
    B, S, D = h.shape
    T = B * S
    xf = h.reshape(T, D)
    logits = jnp.einsum('td,de->te', xf, w_router, preferred_element_type=F32)
    top_logits, top_idx = lax.top_k(logits, TOP_K)
    gates = jax.nn.softmax(top_logits, axis=-1)
    flat_e = top_idx.reshape(-1)
    flat_tok = jnp.repeat(jnp.arange(T, dtype=jnp.int32), TOP_K)
    order = jnp.argsort(flat_e)
    tok_sorted = flat_tok[order]
    xs = xf[tok_sorted]
    group_sizes = jnp.bincount(flat_e, length=N_EXPERTS).astype(jnp.int32)
    hid = jax.nn.silu(lax.ragged_dot(xs, w_gate, group_sizes)) * lax.ragged_dot(xs, w_up, group_sizes)
    ys = lax.ragged_dot(hid, w_down, group_sizes)
    ys = ys * gates.reshape(-1)[order][:, None].astype(ys.dtype)
    out = jnp.zeros_like(xf).at[tok_sorted].add(ys)
    return out.reshape(B, S, D)


def _normal(key, shape, fan_in):
    return jax.random.normal(key, shape, F32) * (fan_in ** -0.5)


def _gain(key, shape):
    return 1.0 + 0.02 * jax.random.normal(key, shape, F32)


def setup_inputs(seed: int = 0) -> dict:
    key = jax.random.key(seed)
    ks = jax.random.split(key, 24)
    x = jax.random.normal(ks[0], (BATCH, SEQ, D_MODEL), F32)
    offset = jax.random.randint(ks[1], (BATCH, 1), 0, 4096, dtype=jnp.int32)
    positions = offset + jnp.arange(SEQ, dtype=jnp.int32)[None, :]
    return {
        "x": x,
        "positions": positions,
        "mix_norm": _gain(ks[2], (DEPTH, D_MODEL)),
        "w_in": _normal(ks[3], (DEPTH, D_MODEL, IN_COLS), D_MODEL),
        "q_a_norm": _gain(ks[4], (DEPTH, Q_RANK)),
        "w_q_b": _normal(ks[5], (DEPTH, Q_RANK, MLA_HEADS * (QK_NOPE + QK_ROPE)), Q_RANK),
        "kv_a_norm": _gain(ks[6], (DEPTH, KV_RANK)),
        "w_kv_b": _normal(ks[7], (DEPTH, KV_RANK, MLA_HEADS * (QK_NOPE + V_HEAD)), KV_RANK),
        "hg_lower_bounds": 0.5 * jax.random.normal(ks[8], (DEPTH, HG_FDIM), F32),
        "hg_out_norm": _gain(ks[9], (DEPTH, HG_VAL)),
        "w_out": _normal(ks[10], (DEPTH, MIX_WIDTH, D_MODEL), MIX_WIDTH),
        "ffn_norm": _gain(ks[11], (DEPTH, D_MODEL)),
        "dense_w_gate": _normal(ks[12], (N_DENSE, D_MODEL, DENSE_FF), D_MODEL),
        "dense_w_up": _normal(ks[13], (N_DENSE, D_MODEL, DENSE_FF), D_MODEL),
        "dense_w_down": _normal(ks[14], (N_DENSE, DENSE_FF, D_MODEL), DENSE_FF),
        "moe_router": _normal(ks[15], (N_MOE, D_MODEL, N_EXPERTS), D_MODEL),
        "moe_w_gate": _normal(ks[16], (N_MOE, N_EXPERTS, D_MODEL, EXPERT_FF), D_MODEL),
        "moe_w_up": _normal(ks[17], (N_MOE, N_EXPERTS, D_MODEL, EXPERT_FF), D_MODEL),
        "moe_w_down": _normal(ks[18], (N_MOE, N_EXPERTS, EXPERT_FF, D_MODEL), EXPERT_FF),
        "final_norm": _gain(ks[19], (D_MODEL,)),
    }


def reference(x, positions, mix_norm, w_in, q_a_norm, w_q_b, kv_a_norm, w_kv_b, hg_lower_bounds, hg_out_norm, w_out, ffn_norm, dense_w_gate, dense_w_up, dense_w_down, moe_router, moe_w_gate, moe_w_up, moe_w_down, final_norm):
    lb_p = jax.nn.softmax(hg_lower_bounds.astype(F32), axis=0)
    lower = jnp.cumsum(lb_p, axis=0) - lb_p[0:1]
    splits = np.cumsum([Q_RANK, KV_RANK, QK_ROPE, HG_FDIM, HG_FDIM, HG_WIDTH]).tolist()
    for l in range(DEPTH):
        h = _rmsnorm(x, mix_norm[l])
        proj = h @ w_in[l]
        c_q, c_kv, k_pe, hq, hf, hi, hg = jnp.split(proj, splits, axis=-1)
        a = _mla(c_q, c_kv, k_pe, positions, q_a_norm[l], w_q_b[l], kv_a_norm[l], w_kv_b[l])
        r = _hgrn2(hq, hf, hi, hg, lower[l], hg_out_norm[l]).astype(x.dtype)
        x = x + jnp.concatenate([a, r], axis=-1) @ w_out[l]
        h = _rmsnorm(x, ffn_norm[l])
        if l % 2 == 0:
            j = l // 2
            x = x + _swiglu(h, dense_w_gate[j], dense_w_up[j], dense_w_down[j])
        else:
            j = l // 2
            x = x + _moe(h, moe_router[j], moe_w_gate[j], moe_w_up[j], moe_w_down[j])
    return _rmsnorm(x, final_norm)
```

```python
import functools

import numpy as np
import jax
import jax.numpy as jnp
from jax import lax
from jax.experimental import pallas as pl
from jax.experimental.pallas import tpu as pltpu

F32 = jnp.float32
BF16 = jnp.bfloat16

D_MODEL = 1024
MLA_HEADS = 4
QK_NOPE = 128
QK_ROPE = 64
QK_DIM = QK_NOPE + QK_ROPE
V_HEAD = 128
Q_RANK = 384
KV_RANK = 256
ROPE_THETA = 10000.0
HG_HEADS = 4
HG_KEY = 128
HG_VAL = 128
HG_CHUNK = 64
HG_SUB = 8
HG_FDIM = HG_HEADS * HG_KEY
HG_WIDTH = HG_HEADS * HG_VAL
MLA_WIDTH = MLA_HEADS * V_HEAD
N_EXPERTS = 8
TOP_K = 2
EPS = 1e-6
F_FLOOR = 1e-30
NEG_BIG = float(np.finfo(np.float32).min)

LANES = 128
VMEM_LIMIT = 56 * 1024 * 1024

TOK_TILE = 512
ATTN_TQ = 512
ATTN_TK = 512
HG_BLOCK = 512
MOE_TM = 512
MOE_FF_SPLIT = 2
ROW_TILE = 256


def _cparams(sem):
    return pltpu.CompilerParams(dimension_semantics=sem, vmem_limit_bytes=VMEM_LIMIT)


def _rms(x, g):
    ms = jnp.mean(x * x, axis=-1, keepdims=True)
    return x * lax.rsqrt(ms + EPS) * g


def _sigmoid(x):
    return 1.0 / (1.0 + jnp.exp(-x))


def _resident(shape):
    nd = len(shape)
    return pl.BlockSpec(shape, lambda *_: (0,) * nd, pipeline_mode=pl.Buffered(1))


def _mix_in_body(x_ref, pos_ref, g_ref, win_ref, qg_ref, wq_ref, kvg_ref, wkv_ref, invf_ref,
                 q_ref, k_ref, v_ref, hqig_ref, hf_ref):
    x = x_ref[0]
    h = _rms(x, g_ref[...]).astype(BF16)

    def proj(a, b):
        return jnp.dot(h, win_ref[:, a:b], preferred_element_type=F32)

    ang = pos_ref[0].astype(F32) * invf_ref[...]
    lane = lax.broadcasted_iota(jnp.int32, ang.shape, 1)
    cs = jnp.where(lane < QK_ROPE, jnp.cos(ang), jnp.sin(ang))

    def rope(pair):
        p = pair * cs
        return p[:, :QK_ROPE] + p[:, QK_ROPE:]

    scale = QK_DIM ** -0.5
    o = 0
    c_q = proj(o, o + Q_RANK); o += Q_RANK
    c_kv = proj(o, o + KV_RANK); o += KV_RANK
    hqig_ref[:, 0:HG_FDIM] = proj(o, o + HG_FDIM).astype(BF16); o += HG_FDIM
    hf_ref[...] = proj(o, o + HG_FDIM); o += HG_FDIM
    hqig_ref[:, HG_FDIM:HG_FDIM + HG_WIDTH] = proj(o, o + HG_WIDTH).astype(BF16); o += HG_WIDTH
    hqig_ref[:, HG_FDIM + HG_WIDTH:] = proj(o, o + HG_WIDTH).astype(BF16); o += HG_WIDTH
    k_pe = rope(proj(o, o + 2 * QK_ROPE)).astype(BF16)

    qf = jnp.dot(_rms(c_q, qg_ref[...]).astype(BF16), wq_ref[...], preferred_element_type=F32)
    kvf = jnp.dot(_rms(c_kv, kvg_ref[...]).astype(BF16), wkv_ref[...], preferred_element_type=F32)
    nw = MLA_HEADS * QK_NOPE
    for hd in range(MLA_HEADS):
        q_ref[0, hd, :, 0:QK_NOPE] = (qf[:, hd * QK_NOPE:(hd + 1) * QK_NOPE] * scale).astype(BF16)
        pe = rope(qf[:, nw + hd * 2 * QK_ROPE: nw + (hd + 1) * 2 * QK_ROPE])
        q_ref[0, hd, :, QK_NOPE:QK_DIM] = (pe * scale).astype(BF16)
        k_ref[0, hd, :, 0:QK_NOPE] = kvf[:, hd * QK_NOPE:(hd + 1) * QK_NOPE].astype(BF16)
        k_ref[0, hd, :, QK_NOPE:QK_DIM] = k_pe
        v_ref[0, hd] = kvf[:, nw + hd * V_HEAD: nw + (hd + 1) * V_HEAD].astype(BF16)


def _mix_in(x, pos3, gain, w_in, q_gain, w_q, kv_gain, w_kv, invf):
    B, S, D = x.shape
    tm = min(TOK_TILE, S)
    ns = S // tm
    T = B * S
    ncol = w_in.shape[1]
    tok = lambda b, i: (b * ns + i, 0)
    return pl.pallas_call(
        _mix_in_body,
        grid=(B, ns),
        in_specs=[
            pl.BlockSpec((1, tm, D), lambda b, i: (b, i, 0)),
            pl.BlockSpec((1, tm, 1), lambda b, i: (b, i, 0)),
            _resident((1, D)),
            _resident((D, ncol)),
            _resident((1, Q_RANK)),
            _resident(w_q.shape),
            _resident((1, KV_RANK)),
            _resident(w_kv.shape),
            _resident((1, LANES)),
        ],
        out_specs=[
            pl.BlockSpec((1, MLA_HEADS, tm, QK_DIM), lambda b, i: (b, 0, i, 0)),
            pl.BlockSpec((1, MLA_HEADS, tm, QK_DIM), lambda b, i: (b, 0, i, 0)),
            pl.BlockSpec((1, MLA_HEADS, tm, V_HEAD), lambda b, i: (b, 0, i, 0)),
            pl.BlockSpec((tm, HG_FDIM + 2 * HG_WIDTH), tok),
            pl.BlockSpec((tm, HG_FDIM), tok),
        ],
        out_shape=[
            jax.ShapeDtypeStruct((B, MLA_HEADS, S, QK_DIM), BF16),
            jax.ShapeDtypeStruct((B, MLA_HEADS, S, QK_DIM), BF16),
            jax.ShapeDtypeStruct((B, MLA_HEADS, S, V_HEAD), BF16),
            jax.ShapeDtypeStruct((T, HG_FDIM + 2 * HG_WIDTH), BF16),
            jax.ShapeDtypeStruct((T, HG_FDIM), F32),
        ],
        compiler_params=_cparams(("parallel", "parallel")),
        name="mix_in",
    )(x, pos3, gain, w_in, q_gain, w_q, kv_gain, w_kv, invf)


def _attn_body(q_ref, k_ref, v_ref, o_ref, *, tq, tk):
    qi = pl.program_id(2)
    q = q_ref[0, 0]

    def block(kb, carry, masked):
        m, l, acc = carry
        start = pl.multiple_of(kb * tk, tk)
        k = k_ref[0, 0, pl.ds(start, tk), :]
        v = v_ref[0, 0, pl.ds(start, tk), :]
        s = lax.dot_general(q, k, (((1,), (1,)), ((), ())), preferred_element_type=F32)
        if masked:
            row = lax.broadcasted_iota(jnp.int32, s.shape, 0)
            col = lax.broadcasted_iota(jnp.int32, s.shape, 1)
            s = jnp.where(col <= row, s, NEG_BIG)
        m_new = jnp.maximum(m, jnp.max(s, axis=-1, keepdims=True))
        alpha = jnp.exp(m - m_new)
        p = jnp.exp(s - m_new)
        l = alpha * l + jnp.sum(p, axis=-1, keepdims=True)
        acc = alpha * acc + jnp.dot(p.astype(BF16), v, preferred_element_type=F32)
        return m_new, l, acc

    init = (jnp.full((tq, 1), NEG_BIG, F32), jnp.zeros((tq, 1), F32), jnp.zeros((tq, V_HEAD), F32))
    carry = lax.fori_loop(0, qi, lambda kb, c: block(kb, c, False), init)
    _, l, acc = block(qi, carry, True)
    o_ref[0] = (acc / l).astype(o_ref.dtype)


def _attention(q, k, v):
    B, H, S, _ = q.shape
    tq = min(ATTN_TQ, S)
    assert ATTN_TQ == ATTN_TK
    return pl.pallas_call(
        functools.partial(_attn_body, tq=tq, tk=tq),
        grid=(B, H, S // tq),
        in_specs=[
            pl.BlockSpec((1, 1, tq, QK_DIM), lambda b, h, i: (b, h, i, 0)),
            pl.BlockSpec((1, 1, S, QK_DIM), lambda b, h, i: (b, h, 0, 0)),
            pl.BlockSpec((1, 1, S, V_HEAD), lambda b, h, i: (b, h, 0, 0)),
        ],
        out_specs=pl.BlockSpec((1, tq, V_HEAD), lambda b, h, i: (b, i, h)),
        out_shape=jax.ShapeDtypeStruct((B, S, H * V_HEAD), BF16),
        compiler_params=_cparams(("parallel", "parallel", "arbitrary")),
        name="mla_attention",
    )(q, k, v)


def _hgrn_body(hqig_ref, hf_ref, lb_ref, gain_ref, r_ref, state_ref, *, n_chunks):
    C = HG_CHUNK

    @pl.when(pl.program_id(1) == 0)
    def _():
        state_ref[...] = jnp.zeros_like(state_ref)

    row = lax.broadcasted_iota(jnp.int32, (C, C), 0)
    col = lax.broadcasted_iota(jnp.int32, (C, C), 1)
    tri = jnp.where(col <= row, 1.0, 0.0).astype(BF16)
    hr = lax.broadcasted_iota(jnp.int32, (HG_FDIM, HG_FDIM), 0) // HG_KEY
    hc = lax.broadcasted_iota(jnp.int32, (HG_FDIM, HG_FDIM), 1) // HG_KEY
    head_ones = jnp.where(hr == hc, 1.0, 0.0).astype(BF16)
    lb = lb_ref[...]
    gain = gain_ref[...]

    levels = []
    L = C // 2
    while L >= HG_SUB:
        levels.append(L)
        L //= 2
    level_masks = [((row // L) % 2 == 1) & ((col // L) % 2 == 0) & (row // (2 * L) == col // (2 * L))
                   for L in levels]
    diag_masks = [(col == (row // HG_SUB) * HG_SUB + s) & (row % HG_SUB >= s) for s in range(HG_SUB)]

    def chunk(c, carry):
        r0 = pl.multiple_of(c * C, C)
        rows = pl.ds(r0, C)
        q_raw = hqig_ref[rows, 0:HG_FDIM].astype(F32)
        z = hf_ref[rows, :]
        q = q_raw * _sigmoid(q_raw)
        f = lb + (1.0 - lb) * _sigmoid(z)
        log_f = jnp.log(jnp.maximum(f, F_FLOOR))
        k = (1.0 - lb) * _sigmoid(-z)

        p0 = log_f.astype(BF16)
        e0 = log_f - p0.astype(F32)
        p1 = e0.astype(BF16)
        p2 = (e0 - p1.astype(F32)).astype(BF16)
        b = (jnp.dot(tri, p0, preferred_element_type=F32)
             + jnp.dot(tri, p1, preferred_element_type=F32)
             + jnp.dot(tri, p2, preferred_element_type=F32))
        b_last = b[C - 1:C, :]
        q_dec = (q * jnp.exp(b)).astype(BF16)
        k_dec = (k * jnp.exp(b_last - b)).astype(BF16)
        s_dec = jnp.exp(b_last)

        nsub = C // HG_SUB
        b3 = b.reshape(nsub, HG_SUB, HG_FDIM)
        q3 = q.reshape(nsub, HG_SUB, HG_FDIM)
        k3 = k.reshape(nsub, HG_SUB, HG_FDIM)
        diag = []
        for s in range(HG_SUB):
            e = jnp.exp(jnp.minimum(b3 - b3[:, s:s + 1, :], 0.0))
            p = (q3 * e * k3[:, s:s + 1, :]).reshape(C, HG_FDIM).astype(BF16)
            diag.append(jnp.dot(p, head_ones, preferred_element_type=F32))

        lvl_q, lvl_k = [], []
        for L in levels:
            g = C // (2 * L)
            bg = b.reshape(g, 2 * L, HG_FDIM)
            ref = jnp.broadcast_to(bg[:, L - 1:L, :], bg.shape).reshape(C, HG_FDIM)
            lvl_q.append((q * jnp.exp(jnp.minimum(b - ref, 0.0))).astype(BF16))
            lvl_k.append((k * jnp.exp(jnp.minimum(ref - b, 0.0))).astype(BF16))

        for hd in range(HG_HEADS):
            lo = hd * HG_KEY
            hs = slice(lo, lo + HG_KEY)
            a = jnp.zeros((C, C), F32)
            for s in range(HG_SUB):
                a = a + jnp.where(diag_masks[s], diag[s][:, lo:lo + C], 0.0)
            for i in range(len(levels)):
                sc = lax.dot_general(lvl_q[i][:, hs], lvl_k[i][:, hs], (((1,), (1,)), ((), ())),
                                     preferred_element_type=F32)
                a = a + jnp.where(level_masks[i], sc, 0.0)
            v = hqig_ref[rows, HG_FDIM + hd * HG_VAL: HG_FDIM + (hd + 1) * HG_VAL]
            st = state_ref[hd]
            o = jnp.dot(a.astype(BF16), v, preferred_element_type=F32)
            o = o + lax.dot_general(q_dec[:, hs], st.astype(BF16), (((1,), (1,)), ((), ())),
                                    preferred_element_type=F32)
            state_ref[hd] = s_dec[:, hs] * st + lax.dot_general(
                v, k_dec[:, hs], (((0,), (0,)), ((), ())), preferred_element_type=F32)
            g_raw = hqig_ref[rows, HG_FDIM + HG_WIDTH + hd * HG_VAL:
                             HG_FDIM + HG_WIDTH + (hd + 1) * HG_VAL].astype(F32)
            y = _rms(o, gain) * (g_raw * _sigmoid(g_raw))
            r_ref[rows, hd * HG_VAL:(hd + 1) * HG_VAL] = y.astype(r_ref.dtype)
        return carry

    lax.fori_loop(0, n_chunks, chunk, 0)


def _hgrn(hqig, hf, lower, out_gain, B, S):
    tb = min(HG_BLOCK, S)
    nb = S // tb
    T = B * S
    tok = lambda b, i: (b * nb + i, 0)
    return pl.pallas_call(
        functools.partial(_hgrn_body, n_chunks=tb // HG_CHUNK),
        grid=(B, nb),
        in_specs=[
            pl.BlockSpec((tb, HG_FDIM + 2 * HG_WIDTH), tok),
            pl.BlockSpec((tb, HG_FDIM), tok),
            _resident((1, HG_FDIM)),
            _resident((1, HG_VAL)),
        ],
        out_specs=pl.BlockSpec((tb, HG_WIDTH), tok),
        out_shape=jax.ShapeDtypeStruct((T, HG_WIDTH), BF16),
        scratch_shapes=[pltpu.VMEM((HG_HEADS, HG_VAL, HG_KEY), F32)],
        compiler_params=_cparams(("parallel", "arbitrary")),
        name="hgrn2_scan",
    )(hqig, hf, lower, out_gain)


def _mix_out(x_ref, a_ref, r_ref, wo_ref):
    y = jnp.dot(a_ref[...], wo_ref[0:MLA_WIDTH, :], preferred_element_type=F32)
    y = y + jnp.dot(r_ref[...], wo_ref[MLA_WIDTH:, :], preferred_element_type=F32)
    return x_ref[...] + y


def _dense_body(x_ref, a_ref, r_ref, wo_ref, g_ref, wg_ref, wu_ref, wd_ref, *rest, n_split, final):
    if final:
        fg_ref, o_ref = rest
    else:
        (o_ref,) = rest
    x1 = _mix_out(x_ref, a_ref, r_ref, wo_ref)
    h = _rms(x1, g_ref[...]).astype(BF16)
    ff = wg_ref.shape[1]
    step = ff // n_split
    y = x1
    for j in range(n_split):
        cs = slice(j * step, (j + 1) * step)
        gate = jnp.dot(h, wg_ref[:, cs], preferred_element_type=F32)
        up = jnp.dot(h, wu_ref[:, cs], preferred_element_type=F32)
        act = (gate * _sigmoid(gate) * up).astype(BF16)
        y = y + jnp.dot(act, wd_ref[cs, :], preferred_element_type=F32)
    if final:
        y = _rms(y, fg_ref[...])
    o_ref[...] = y


def _dense_layer(x2, a, r, w_out, gain, wg, wu, wd, final_gain):
    T, D = x2.shape
    tm = min(TOK_TILE, T)
    ff = wg.shape[1]
    tok = lambda i: (i, 0)
    final = final_gain is not None
    in_specs = [
        pl.BlockSpec((tm, D), tok),
        pl.BlockSpec((tm, MLA_WIDTH), tok),
        pl.BlockSpec((tm, HG_WIDTH), tok),
        _resident(w_out.shape),
        _resident((1, D)),
        _resident((D, ff)),
        _resident((D, ff)),
        _resident((ff, D)),
    ]
    args = [x2, a, r, w_out, gain, wg, wu, wd]
    if final:
        in_specs.append(_resident((1, D)))
        args.append(final_gain)
    return pl.pallas_call(
        functools.partial(_dense_body, n_split=2, final=final),
        grid=(T // tm,),
        in_specs=in_specs,
        out_specs=pl.BlockSpec((tm, D), tok),
        out_shape=jax.ShapeDtypeStruct((T, D), F32),
        compiler_params=_cparams(("parallel",)),
        name="mix_out_dense_ffn",
    )(*args)


def _router_body(x_ref, a_ref, r_ref, wo_ref, g_ref, wr_ref, x1_ref, h_ref, route_ref):
    x1 = _mix_out(x_ref, a_ref, r_ref, wo_ref)
    x1_ref[...] = x1
    h = _rms(x1, g_ref[...])
    h_ref[...] = h
    logits = jnp.dot(h, wr_ref[...], preferred_element_type=F32, precision=lax.Precision.HIGHEST)
    lane = lax.broadcasted_iota(jnp.int32, logits.shape, 1)
    neg = -jnp.inf
    lg = jnp.where(lane < N_EXPERTS, logits, neg)
    m1 = jnp.max(lg, axis=-1, keepdims=True)
    i1 = jnp.min(jnp.where(lg == m1, lane, LANES), axis=-1, keepdims=True)
    lg2 = jnp.where(lane == i1, neg, lg)
    m2 = jnp.max(lg2, axis=-1, keepdims=True)
    i2 = jnp.min(jnp.where(lg2 == m2, lane, LANES), axis=-1, keepdims=True)
    e2 = jnp.exp(m2 - m1)
    g1 = 1.0 / (1.0 + e2)
    g2 = e2 / (1.0 + e2)
    route = jnp.where(lane == 0, i1.astype(F32),
                      jnp.where(lane == 1, i2.astype(F32),
                                jnp.where(lane == 2, g1, jnp.where(lane == 3, g2, 0.0))))
    route_ref[...] = route


def _router(x2, a, r, w_out, gain, w_router):
    T, D = x2.shape
    tm = min(TOK_TILE, T)
    tok = lambda i: (i, 0)
    return pl.pallas_call(
        _router_body,
        grid=(T // tm,),
        in_specs=[
            pl.BlockSpec((tm, D), tok),
            pl.BlockSpec((tm, MLA_WIDTH), tok),
            pl.BlockSpec((tm, HG_WIDTH), tok),
            _resident(w_out.shape),
            _resident((1, D)),
            _resident((D, LANES)),
        ],
        out_specs=[pl.BlockSpec((tm, D), tok), pl.BlockSpec((tm, D), tok), pl.BlockSpec((tm, LANES), tok)],
        out_shape=[jax.ShapeDtypeStruct((T, D), F32), jax.ShapeDtypeStruct((T, D), F32),
                   jax.ShapeDtypeStruct((T, LANES), F32)],
        compiler_params=_cparams(("parallel",)),
        name="mix_out_router",
    )(x2, a, r, w_out, gain, w_router)


def _dispatch_body(pos_ref, fill_ref, h_ref, xs_ref, zero_ref, sem, *, tm, tile):
    i = pl.program_id(0)

    @pl.when(i == 0)
    def _():
        zero_ref[...] = jnp.zeros_like(zero_ref)
        for e in range(2 * N_EXPERTS):
            @pl.when(fill_ref[e] >= 0)
            def _():
                cp = pltpu.make_async_copy(zero_ref, xs_ref.at[pl.ds(pl.multiple_of(fill_ref[e], tile), tile), :], sem)
                cp.start()
                cp.wait()

    def row_copy(t, k):
        dst = pos_ref[(i * tm + t) * TOP_K + k]
        return pltpu.make_async_copy(h_ref.at[pl.ds(t, 1), :], xs_ref.at[pl.ds(dst, 1), :], sem)

    def issue(t, c):
        for k in range(TOP_K):
            row_copy(t, k).start()
        return c

    def drain(t, c):
        for k in range(TOP_K):
            row_copy(t, k).wait()
        return c

    lax.fori_loop(0, tm, issue, 0)
    lax.fori_loop(0, tm, drain, 0)


def _dispatch(pos_flat, fill_start, h, n_rows, tile):
    T, D = h.shape
    tm = min(ROW_TILE, T)
    grid_spec = pltpu.PrefetchScalarGridSpec(
        num_scalar_prefetch=2,
        grid=(T // tm,),
        in_specs=[pl.BlockSpec((tm, D), lambda i, *_: (i, 0))],
        out_specs=pl.BlockSpec(memory_space=pl.ANY),
        scratch_shapes=[pltpu.VMEM((tile, D), F32), pltpu.SemaphoreType.DMA(())],
    )
    return pl.pallas_call(
        functools.partial(_dispatch_body, tm=tm, tile=tile),
        grid_spec=grid_spec,
        out_shape=jax.ShapeDtypeStruct((n_rows, D), F32),
        compiler_params=_cparams(("arbitrary",)),
        name="moe_dispatch",
    )(pos_flat, fill_start, h)


def _expert_body(te_ref, nv_ref, xs_ref, wg_ref, wu_ref, wd_ref, ys_ref, acc_ref):
    t = pl.program_id(0)
    j = pl.program_id(1)

    @pl.when(t < nv_ref[0])
    def _():
        x = xs_ref[...].astype(BF16)
        gate = jnp.dot(x, wg_ref[0], preferred_element_type=F32)
        up = jnp.dot(x, wu_ref[0], preferred_element_type=F32)
        act = (gate * _sigmoid(gate) * up).astype(BF16)
        y = jnp.dot(act, wd_ref[0], preferred_element_type=F32)

        @pl.when(j == 0)
        def _():
            acc_ref[...] = y

        @pl.when(j > 0)
        def _():
            acc_ref[...] += y

        @pl.when(j == pl.num_programs(1) - 1)
        def _():
            ys_ref[...] = acc_ref[...]

    @pl.when(t >= nv_ref[0])
    def _():
        ys_ref[...] = jnp.zeros_like(ys_ref)


def _experts(tile_expert, n_valid, xs, wg, wu, wd, tile):
    P, D = xs.shape
    ff = wg.shape[2]
    fs = ff // MOE_FF_SPLIT
    n_tiles = P // tile

    def row_map(t, j, te, nv):
        return (jnp.minimum(t, nv[0] - 1), 0)

    grid_spec = pltpu.PrefetchScalarGridSpec(
        num_scalar_prefetch=2,
        grid=(n_tiles, MOE_FF_SPLIT),
        in_specs=[
            pl.BlockSpec((tile, D), row_map),
            pl.BlockSpec((1, D, fs), lambda t, j, te, nv: (te[t], 0, j)),
            pl.BlockSpec((1, D, fs), lambda t, j, te, nv: (te[t], 0, j)),
            pl.BlockSpec((1, fs, D), lambda t, j, te, nv: (te[t], j, 0)),
        ],
        out_specs=pl.BlockSpec((tile, D), lambda t, j, te, nv: (t, 0)),
        scratch_shapes=[pltpu.VMEM((tile, D), F32)],
    )
    return pl.pallas_call(
        _expert_body,
        grid_spec=grid_spec,
        out_shape=jax.ShapeDtypeStruct((P, D), F32),
        compiler_params=_cparams(("arbitrary", "arbitrary")),
        name="moe_experts",
    )(tile_expert, n_valid, xs, wg, wu, wd)


def _combine_body(pos_ref, x1_ref, route_ref, ys_ref, *rest, tm, final):
    if final:
        fg_ref, o_ref, y0_ref, y1_ref, sem = rest
    else:
        o_ref, y0_ref, y1_ref, sem = rest
    i = pl.program_id(0)
    bufs = (y0_ref, y1_ref)

    def row_copy(t, k):
        src = pos_ref[(i * tm + t) * TOP_K + k]
        return pltpu.make_async_copy(ys_ref.at[pl.ds(src, 1), :], bufs[k].at[pl.ds(t, 1), :], sem)

    def issue(t, c):
        for k in range(TOP_K):
            row_copy(t, k).start()
        return c

    def drain(t, c):
        for k in range(TOP_K):
            row_copy(t, k).wait()
        return c

    lax.fori_loop(0, tm, issue, 0)
    lax.fori_loop(0, tm, drain, 0)
    route = route_ref[...]
    y = x1_ref[...] + route[:, 2:3] * y0_ref[...] + route[:, 3:4] * y1_ref[...]
    if final:
        y = _rms(y, fg_ref[...])
    o_ref[...] = y


def _combine(pos_flat, x1, route, ys, final_gain):
    T, D = x1.shape
    tm = min(ROW_TILE, T)
    final = final_gain is not None
    tok = lambda i, *_: (i, 0)
    in_specs = [
        pl.BlockSpec((tm, D), tok),
        pl.BlockSpec((tm, LANES), tok),
        pl.BlockSpec(memory_space=pl.ANY),
    ]
    args = [pos_flat, x1, route, ys]
    if final:
        in_specs.append(pl.BlockSpec((1, D), lambda i, *_: (0, 0)))
        args.append(final_gain)
    grid_spec = pltpu.PrefetchScalarGridSpec(
        num_scalar_prefetch=1,
        grid=(T // tm,),
        in_specs=in_specs,
        out_specs=pl.BlockSpec((tm, D), tok),
        scratch_shapes=[pltpu.VMEM((tm, D), F32), pltpu.VMEM((tm, D), F32), pltpu.SemaphoreType.DMA(())],
    )
    return pl.pallas_call(
        functools.partial(_combine_body, tm=tm, final=final),
        grid_spec=grid_spec,
        out_shape=jax.ShapeDtypeStruct((T, D), F32),
        compiler_params=_cparams(("arbitrary",)),
        name="moe_combine",
    )(*args)


def _moe_layer(x2, a, r, w_out, gain, w_router, wg, wu, wd, final_gain):
    T, D = x2.shape
    tile = min(MOE_TM, T)
    wr = jnp.zeros((D, LANES), F32).at[:, :N_EXPERTS].set(w_router)
    x1, h, route = _router(x2, a, r, w_out, gain, wr)

    n_slots = T * TOP_K
    n_rows = n_slots + N_EXPERTS * tile
    flat_e = route[:, :TOP_K].astype(jnp.int32).reshape(-1)
    onehot = (flat_e[:, None] == jnp.arange(N_EXPERTS, dtype=jnp.int32)[None, :]).astype(jnp.int32)
    csum = jnp.cumsum(onehot, axis=0)
    rank = jnp.sum(onehot * (csum - onehot), axis=1)
    counts = csum[-1]
    padded = ((counts + tile - 1) // tile) * tile
    ends = jnp.cumsum(padded)
    starts = ends - padded
    pos_flat = (jnp.sum(onehot * starts[None, :], axis=1) + rank).astype(jnp.int32)
    tail_row = n_rows - tile * jnp.arange(1, N_EXPERTS + 1, dtype=jnp.int32)
    fill_start = jnp.concatenate([jnp.where(padded > 0, ends - tile, -1),
                                  jnp.where(tail_row >= ends[-1], tail_row, -1)]).astype(jnp.int32)
    tile_row = jnp.arange(n_rows // tile, dtype=jnp.int32) * tile
    tile_expert = jnp.minimum(jnp.sum(tile_row[:, None] >= ends[None, :], axis=1), N_EXPERTS - 1).astype(jnp.int32)
    n_valid = (ends[-1:] // tile).astype(jnp.int32)

    xs = _dispatch(pos_flat, fill_start, h, n_rows, tile)
    ys = _experts(tile_expert, n_valid, xs, wg, wu, wd, tile)
    return _combine(pos_flat, x1, route, ys, final_gain)


def _rotate_half_cols(w):
    half = QK_ROPE // 2
    return jnp.concatenate([-w[..., half:], w[..., :half]], axis=-1)


def _prep_w_in(w):
    o = Q_RANK + KV_RANK
    k_pe = w[:, o:o + QK_ROPE]
    return jnp.concatenate([w[:, :o], w[:, o + QK_ROPE:], k_pe, _rotate_half_cols(k_pe)], axis=1).astype(BF16)


def _prep_w_q(w):
    w3 = w.reshape(Q_RANK, MLA_HEADS, QK_DIM)
    nope = w3[:, :, :QK_NOPE].reshape(Q_RANK, MLA_HEADS * QK_NOPE)
    pe = w3[:, :, QK_NOPE:]
    pair = jnp.concatenate([pe, _rotate_half_cols(pe)], axis=-1).reshape(Q_RANK, MLA_HEADS * 2 * QK_ROPE)
    return jnp.concatenate([nope, pair], axis=1).astype(BF16)


def _prep_w_kv(w):
    w4 = w.reshape(KV_RANK, MLA_HEADS, 2, QK_NOPE)
    return w4.transpose(0, 2, 1, 3).reshape(KV_RANK, 2 * MLA_HEADS * QK_NOPE).astype(BF16)


def kernel(x, positions, mix_norm, w_in, q_a_norm, w_q_b, kv_a_norm, w_kv_b, hg_lower_bounds, hg_out_norm, w_out, ffn_norm, dense_w_gate, dense_w_up, dense_w_down, moe_router, moe_w_gate, moe_w_up, moe_w_down, final_norm):
    B, S, D = x.shape
    depth = w_in.shape[0]
    lb_p = jax.nn.softmax(hg_lower_bounds.astype(F32), axis=0)
    lower = jnp.cumsum(lb_p, axis=0) - lb_p[0:1]
    inv = ROPE_THETA ** (-jnp.arange(0, QK_ROPE, 2, dtype=F32) / QK_ROPE)
    invf = jnp.tile(inv, 2 * QK_ROPE // inv.shape[0]).reshape(1, LANES)
    pos3 = positions.reshape(B, S, 1)
    final_gain = final_norm.reshape(1, D)

    x2 = x.reshape(B * S, D)
    for l in range(depth):
        q, k, v, hqig, hf = _mix_in(
            x2.reshape(B, S, D), pos3, mix_norm[l].reshape(1, D), _prep_w_in(w_in[l]),
            q_a_norm[l].reshape(1, Q_RANK), _prep_w_q(w_q_b[l]),
            kv_a_norm[l].reshape(1, KV_RANK), _prep_w_kv(w_kv_b[l]), invf)
        a = _attention(q, k, v).reshape(B * S, MLA_WIDTH)
        r = _hgrn(hqig, hf, lower[l].reshape(1, HG_FDIM), hg_out_norm[l].reshape(1, HG_VAL), B, S)
        fg = final_gain if l == depth - 1 else None
        wo = w_out[l].astype(BF16)
        gain = ffn_norm[l].reshape(1, D)
        j = l // 2
        if l % 2 == 0:
            x2 = _dense_layer(x2, a, r, wo, gain, dense_w_gate[j].astype(BF16), dense_w_up[j].astype(BF16),
                              dense_w_down[j].astype(BF16), fg)
        else:
            x2 = _moe_layer(x2, a, r, wo, gain, moe_router[j], moe_w_gate[j].astype(BF16),
                            moe_w_up[j].astype(BF16), moe_w_down[j].astype(BF16), fg)
    return x2.reshape(B, S, D)
```

```python
import functools

import numpy as np
import jax
import jax.numpy as jnp
from jax import lax
from jax.experimental import pallas as pl
from jax.experimental.pallas import tpu as pltpu

F32 = jnp.float32
BF16 = jnp.bfloat16

D_MODEL = 1024
MLA_HEADS = 4
QK_NOPE = 128
QK_ROPE = 64
QK_DIM = QK_NOPE + QK_ROPE
V_HEAD = 128
Q_RANK = 384
KV_RANK = 256
ROPE_THETA = 10000.0
HG_HEADS = 4
HG_KEY = 128
HG_VAL = 128
HG_CHUNK = 64
HG_SUB = 8
HG_FDIM = HG_HEADS * HG_KEY
HG_WIDTH = HG_HEADS * HG_VAL
MLA_WIDTH = MLA_HEADS * V_HEAD
N_EXPERTS = 8
TOP_K = 2
EPS = 1e-6
F_FLOOR = 1e-30
NEG_BIG = float(np.finfo(np.float32).min)
LOG2_E = float(np.log2(np.e))

LANES = 128
VMEM_LIMIT = 56 * 1024 * 1024

TOK_TILE = 512
ATTN_HEADS_PER_STEP = 2
HG_BLOCK = 512
MOE_TM = 512
MOE_FF_SPLIT = 2
ROW_TILE = 256


def _cparams(sem):
    return pltpu.CompilerParams(dimension_semantics=sem, vmem_limit_bytes=VMEM_LIMIT)


def _rms(x, g):
    ms = jnp.mean(x * x, axis=-1, keepdims=True)
    return x * lax.rsqrt(ms + EPS) * g


def _sigmoid(x):
    return 1.0 / (1.0 + jnp.exp(-x))


def _resident(shape):
    nd = len(shape)
    return pl.BlockSpec(shape, lambda *_: (0,) * nd, pipeline_mode=pl.Buffered(1))


def _mix_in_body(x_ref, pos_ref, g_ref, win_ref, qg_ref, wq_ref, kvg_ref, wkv_ref, invf_ref,
                 q_ref, k_ref, v_ref, hqig_ref, hf_ref):
    x = x_ref[0]
    h = _rms(x, g_ref[...]).astype(BF16)

    def proj(a, b):
        return jnp.dot(h, win_ref[:, a:b], preferred_element_type=F32)

    ang = pos_ref[0].astype(F32) * invf_ref[...]
    lane = lax.broadcasted_iota(jnp.int32, ang.shape, 1)
    cs = jnp.where(lane < QK_ROPE, jnp.cos(ang), jnp.sin(ang))

    def rope(pair):
        p = pair * cs
        return p[:, :QK_ROPE] + p[:, QK_ROPE:]

    scale = QK_DIM ** -0.5 * LOG2_E
    o = 0
    c_q = proj(o, o + Q_RANK); o += Q_RANK
    c_kv = proj(o, o + KV_RANK); o += KV_RANK
    hqig_ref[:, 0:HG_FDIM] = proj(o, o + HG_FDIM).astype(BF16); o += HG_FDIM
    hf_ref[...] = proj(o, o + HG_FDIM); o += HG_FDIM
    hqig_ref[:, HG_FDIM:HG_FDIM + HG_WIDTH] = proj(o, o + HG_WIDTH).astype(BF16); o += HG_WIDTH
    hqig_ref[:, HG_FDIM + HG_WIDTH:] = proj(o, o + HG_WIDTH).astype(BF16); o += HG_WIDTH
    k_pe = rope(proj(o, o + 2 * QK_ROPE)).astype(BF16)

    tn = (((0,), (1,)), ((), ()))
    nw = MLA_HEADS * QK_NOPE
    qn = _rms(c_q, qg_ref[...]).astype(BF16)
    kvn = _rms(c_kv, kvg_ref[...]).astype(BF16)
    qf_t = lax.dot_general(wq_ref[...], qn, tn, preferred_element_type=F32)
    kf = jnp.dot(kvn, wkv_ref[:, 0:nw], preferred_element_type=F32)
    vf_t = lax.dot_general(wkv_ref[:, nw:], kvn, tn, preferred_element_type=F32)
    cs_t = cs.T
    for hd in range(MLA_HEADS):
        q_ref[0, hd, 0:QK_NOPE, :] = (qf_t[hd * QK_NOPE:(hd + 1) * QK_NOPE, :] * scale).astype(BF16)
        pr = qf_t[nw + hd * 2 * QK_ROPE: nw + (hd + 1) * 2 * QK_ROPE, :] * cs_t
        q_ref[0, hd, QK_NOPE:QK_DIM, :] = ((pr[:QK_ROPE, :] + pr[QK_ROPE:, :]) * scale).astype(BF16)
        k_ref[0, hd, :, 0:QK_NOPE] = kf[:, hd * QK_NOPE:(hd + 1) * QK_NOPE].astype(BF16)
        k_ref[0, hd, :, QK_NOPE:QK_DIM] = k_pe
        v_ref[0, hd, 0] = vf_t[hd * V_HEAD:(hd + 1) * V_HEAD, :].astype(BF16)


def _mix_in(x, pos3, gain, w_in, q_gain, w_q, kv_gain, w_kv, invf):
    B, S, D = x.shape
    tm = min(TOK_TILE, S)
    ns = S // tm
    T = B * S
    ncol = w_in.shape[1]
    tok = lambda b, i: (b * ns + i, 0)
    return pl.pallas_call(
        _mix_in_body,
        grid=(B, ns),
        in_specs=[
            pl.BlockSpec((1, tm, D), lambda b, i: (b, i, 0)),
            pl.BlockSpec((1, tm, 1), lambda b, i: (b, i, 0)),
            _resident((1, D)),
            _resident((D, ncol)),
            _resident((1, Q_RANK)),
            _resident(w_q.shape),
            _resident((1, KV_RANK)),
            _resident(w_kv.shape),
            _resident((1, LANES)),
        ],
        out_specs=[
            pl.BlockSpec((1, MLA_HEADS, QK_DIM, tm), lambda b, i: (b, 0, 0, i)),
            pl.BlockSpec((1, MLA_HEADS, tm, QK_DIM), lambda b, i: (b, 0, i, 0)),
            pl.BlockSpec((1, MLA_HEADS, 1, V_HEAD, tm), lambda b, i: (b, 0, i, 0, 0)),
            pl.BlockSpec((tm, HG_FDIM + 2 * HG_WIDTH), tok),
            pl.BlockSpec((tm, HG_FDIM), tok),
        ],
        out_shape=[
            jax.ShapeDtypeStruct((B, MLA_HEADS, QK_DIM, S), BF16),
            jax.ShapeDtypeStruct((B, MLA_HEADS, S, QK_DIM), BF16),
            jax.ShapeDtypeStruct((B, MLA_HEADS, ns, V_HEAD, tm), BF16),
            jax.ShapeDtypeStruct((T, HG_FDIM + 2 * HG_WIDTH), BF16),
            jax.ShapeDtypeStruct((T, HG_FDIM), F32),
        ],
        compiler_params=_cparams(("parallel", "parallel")),
        name="mix_in",
    )(x, pos3, gain, w_in, q_gain, w_q, kv_gain, w_kv, invf)


def _attn_body(q_ref, k_ref, v_ref, o_ref, s0_ref, s1_ref, m_ref, l_ref, acc_ref, *, tq, tk, nh):
    qi = pl.program_id(2)

    heads = range(nh)

    def scores(kb, s_ref):
        start = pl.multiple_of(kb * tk, tk)
        for hd in heads:
            s_ref[hd] = jnp.dot(k_ref[0, hd, pl.ds(start, tk), :], q_ref[0, hd],
                                preferred_element_type=F32)

    def update(kb, s_of_head):
        for hd in heads:
            s = s_of_head(hd)
            m = m_ref[hd]
            m_new = jnp.maximum(m, jnp.max(s, axis=0, keepdims=True))
            alpha = jnp.exp2(m - m_new)
            p = jnp.exp2(s - m_new)
            m_ref[hd] = m_new
            l_ref[hd] = alpha * l_ref[hd] + jnp.sum(p, axis=0, keepdims=True)
            acc_ref[hd] = alpha * acc_ref[hd] + jnp.dot(v_ref[0, hd, kb], p.astype(BF16),
                                                        preferred_element_type=F32)

    def diagonal_scores(hd):
        start = pl.multiple_of(qi * tk, tk)
        s = jnp.dot(k_ref[0, hd, pl.ds(start, tk), :], q_ref[0, hd], preferred_element_type=F32)
        key = lax.broadcasted_iota(jnp.int32, s.shape, 0)
        qry = lax.broadcasted_iota(jnp.int32, s.shape, 1)
        return jnp.where(key <= qry, s, NEG_BIG)

    m_ref[...] = jnp.full(m_ref.shape, NEG_BIG, F32)
    l_ref[...] = jnp.zeros(l_ref.shape, F32)
    acc_ref[...] = jnp.zeros(acc_ref.shape, F32)
    scores(0, s0_ref)
    update(qi, diagonal_scores)

    def pair(j, carry):
        scores(2 * j + 1, s1_ref)
        update(2 * j, lambda hd: s0_ref[hd])
        scores(2 * j + 2, s0_ref)
        update(2 * j + 1, lambda hd: s1_ref[hd])
        return carry

    lax.fori_loop(0, qi // 2, pair, 0)

    @pl.when(qi % 2 == 1)
    def _():
        update(qi - 1, lambda hd: s0_ref[hd])

    for hd in heads:
        o_ref[0, :, hd * V_HEAD:(hd + 1) * V_HEAD] = (acc_ref[hd] / l_ref[hd]).T.astype(o_ref.dtype)


def _attention(q, k, v):
    B, H, S, _ = k.shape
    tq = v.shape[-1]
    nh = ATTN_HEADS_PER_STEP
    return pl.pallas_call(
        functools.partial(_attn_body, tq=tq, tk=tq, nh=nh),
        grid=(B, H // nh, S // tq),
        in_specs=[
            pl.BlockSpec((1, nh, QK_DIM, tq), lambda b, h, i: (b, h, 0, i)),
            pl.BlockSpec((1, nh, S, QK_DIM), lambda b, h, i: (b, h, 0, 0)),
            pl.BlockSpec((1, nh, S // tq, V_HEAD, tq), lambda b, h, i: (b, h, 0, 0, 0)),
        ],
        out_specs=pl.BlockSpec((1, tq, nh * V_HEAD), lambda b, h, i: (b, i, h)),
        out_shape=jax.ShapeDtypeStruct((B, S, H * V_HEAD), BF16),
        scratch_shapes=[
            pltpu.VMEM((nh, tq, tq), F32),
            pltpu.VMEM((nh, tq, tq), F32),
            pltpu.VMEM((nh, 1, tq), F32),
            pltpu.VMEM((nh, 1, tq), F32),
            pltpu.VMEM((nh, V_HEAD, tq), F32),
        ],
        compiler_params=_cparams(("parallel", "parallel", "arbitrary")),
        name="mla_attention",
    )(q, k, v)


def _hgrn_body(hqig_ref, hf_ref, lb_ref, gain_ref, r_ref, state_ref, *, n_chunks):
    C = HG_CHUNK

    @pl.when(pl.program_id(1) == 0)
    def _():
        state_ref[...] = jnp.zeros_like(state_ref)

    row = lax.broadcasted_iota(jnp.int32, (C, C), 0)
    col = lax.broadcasted_iota(jnp.int32, (C, C), 1)
    tri = jnp.where(col <= row, 1.0, 0.0).astype(BF16)
    hr = lax.broadcasted_iota(jnp.int32, (HG_FDIM, HG_FDIM), 0) // HG_KEY
    hc = lax.broadcasted_iota(jnp.int32, (HG_FDIM, HG_FDIM), 1) // HG_KEY
    head_ones = jnp.where(hr == hc, 1.0, 0.0).astype(BF16)
    lb = lb_ref[...]
    gain = gain_ref[...]

    levels = []
    L = C // 2
    while L >= HG_SUB:
        levels.append(L)
        L //= 2
    level_masks = [((row // L) % 2 == 1) & ((col // L) % 2 == 0) & (row // (2 * L) == col // (2 * L))
                   for L in levels]
    diag_masks = [(col == (row // HG_SUB) * HG_SUB + s) & (row % HG_SUB >= s) for s in range(HG_SUB)]

    def chunk(c, carry):
        r0 = pl.multiple_of(c * C, C)
        rows = pl.ds(r0, C)
        q_raw = hqig_ref[rows, 0:HG_FDIM].astype(F32)
        z = hf_ref[rows, :]
        q = q_raw * _sigmoid(q_raw)
        f = lb + (1.0 - lb) * _sigmoid(z)
        log_f = jnp.log(jnp.maximum(f, F_FLOOR))
        k = (1.0 - lb) * _sigmoid(-z)

        p0 = log_f.astype(BF16)
        e0 = log_f - p0.astype(F32)
        p1 = e0.astype(BF16)
        p2 = (e0 - p1.astype(F32)).astype(BF16)
        b = (jnp.dot(tri, p0, preferred_element_type=F32)
             + jnp.dot(tri, p1, preferred_element_type=F32)
             + jnp.dot(tri, p2, preferred_element_type=F32))
        b_last = b[C - 1:C, :]
        q_dec = (q * jnp.exp(b)).astype(BF16)
        k_dec = (k * jnp.exp(b_last - b)).astype(BF16)
        s_dec = jnp.exp(b_last)

        nsub = C // HG_SUB
        b3 = b.reshape(nsub, HG_SUB, HG_FDIM)
        q3 = q.reshape(nsub, HG_SUB, HG_FDIM)
        k3 = k.reshape(nsub, HG_SUB, HG_FDIM)
        diag = []
        for s in range(HG_SUB):
            e = jnp.exp(jnp.minimum(b3 - b3[:, s:s + 1, :], 0.0))
            p = (q3 * e * k3[:, s:s + 1, :]).reshape(C, HG_FDIM).astype(BF16)
            diag.append(jnp.dot(p, head_ones, preferred_element_type=F32))

        lvl_q, lvl_k = [], []
        for L in levels:
            g = C // (2 * L)
            bg = b.reshape(g, 2 * L, HG_FDIM)
            ref = jnp.broadcast_to(bg[:, L - 1:L, :], bg.shape).reshape(C, HG_FDIM)
            lvl_q.append((q * jnp.exp(jnp.minimum(b - ref, 0.0))).astype(BF16))
            lvl_k.append((k * jnp.exp(jnp.minimum(ref - b, 0.0))).astype(BF16))

        for hd in range(HG_HEADS):
            lo = hd * HG_KEY
            hs = slice(lo, lo + HG_KEY)
            a = jnp.zeros((C, C), F32)
            for s in range(HG_SUB):
                a = a + jnp.where(diag_masks[s], diag[s][:, lo:lo + C], 0.0)
            for i in range(len(levels)):
                sc = lax.dot_general(lvl_q[i][:, hs], lvl_k[i][:, hs], (((1,), (1,)), ((), ())),
                                     preferred_element_type=F32)
                a = a + jnp.where(level_masks[i], sc, 0.0)
            v = hqig_ref[rows, HG_FDIM + hd * HG_VAL: HG_FDIM + (hd + 1) * HG_VAL]
            st = state_ref[hd]
            o = jnp.dot(a.astype(BF16), v, preferred_element_type=F32)
            o = o + lax.dot_general(q_dec[:, hs], st.astype(BF16), (((1,), (1,)), ((), ())),
                                    preferred_element_type=F32)
            state_ref[hd] = s_dec[:, hs] * st + lax.dot_general(
                v, k_dec[:, hs], (((0,), (0,)), ((), ())), preferred_element_type=F32)
            g_raw = hqig_ref[rows, HG_FDIM + HG_WIDTH + hd * HG_VAL:
                             HG_FDIM + HG_WIDTH + (hd + 1) * HG_VAL].astype(F32)
            y = _rms(o, gain) * (g_raw * _sigmoid(g_raw))
            r_ref[rows, hd * HG_VAL:(hd + 1) * HG_VAL] = y.astype(r_ref.dtype)
        return carry

    lax.fori_loop(0, n_chunks, chunk, 0)


def _hgrn(hqig, hf, lower, out_gain, B, S):
    tb = min(HG_BLOCK, S)
    nb = S // tb
    T = B * S
    tok = lambda b, i: (b * nb + i, 0)
    return pl.pallas_call(
        functools.partial(_hgrn_body, n_chunks=tb // HG_CHUNK),
        grid=(B, nb),
        in_specs=[
            pl.BlockSpec((tb, HG_FDIM + 2 * HG_WIDTH), tok),
            pl.BlockSpec((tb, HG_FDIM), tok),
            _resident((1, HG_FDIM)),
            _resident((1, HG_VAL)),
        ],
        out_specs=pl.BlockSpec((tb, HG_WIDTH), tok),
        out_shape=jax.ShapeDtypeStruct((T, HG_WIDTH), BF16),
        scratch_shapes=[pltpu.VMEM((HG_HEADS, HG_VAL, HG_KEY), F32)],
        compiler_params=_cparams(("parallel", "arbitrary")),
        name="hgrn2_scan",
    )(hqig, hf, lower, out_gain)


def _mix_out(x_ref, a_ref, r_ref, wo_ref):
    y = jnp.dot(a_ref[...], wo_ref[0:MLA_WIDTH, :], preferred_element_type=F32)
    y = y + jnp.dot(r_ref[...], wo_ref[MLA_WIDTH:, :], preferred_element_type=F32)
    return x_ref[...] + y


def _dense_body(x_ref, a_ref, r_ref, wo_ref, g_ref, wg_ref, wu_ref, wd_ref, *rest, n_split, final):
    if final:
        fg_ref, o_ref = rest
    else:
        (o_ref,) = rest
    x1 = _mix_out(x_ref, a_ref, r_ref, wo_ref)
    h = _rms(x1, g_ref[...]).astype(BF16)
    ff = wg_ref.shape[1]
    step = ff // n_split
    y = x1
    for j in range(n_split):
        cs = slice(j * step, (j + 1) * step)
        gate = jnp.dot(h, wg_ref[:, cs], preferred_element_type=F32)
        up = jnp.dot(h, wu_ref[:, cs], preferred_element_type=F32)
        act = (gate * _sigmoid(gate) * up).astype(BF16)
        y = y + jnp.dot(act, wd_ref[cs, :], preferred_element_type=F32)
    if final:
        y = _rms(y, fg_ref[...])
    o_ref[...] = y


def _dense_layer(x2, a, r, w_out, gain, wg, wu, wd, final_gain):
    T, D = x2.shape
    tm = min(TOK_TILE, T)
    ff = wg.shape[1]
    tok = lambda i: (i, 0)
    final = final_gain is not None
    in_specs = [
        pl.BlockSpec((tm, D), tok),
        pl.BlockSpec((tm, MLA_WIDTH), tok),
        pl.BlockSpec((tm, HG_WIDTH), tok),
        _resident(w_out.shape),
        _resident((1, D)),
        _resident((D, ff)),
        _resident((D, ff)),
        _resident((ff, D)),
    ]
    args = [x2, a, r, w_out, gain, wg, wu, wd]
    if final:
        in_specs.append(_resident((1, D)))
        args.append(final_gain)
    return pl.pallas_call(
        functools.partial(_dense_body, n_split=2, final=final),
        grid=(T // tm,),
        in_specs=in_specs,
        out_specs=pl.BlockSpec((tm, D), tok),
        out_shape=jax.ShapeDtypeStruct((T, D), F32),
        compiler_params=_cparams(("parallel",)),
        name="mix_out_dense_ffn",
    )(*args)


def _router_body(x_ref, a_ref, r_ref, wo_ref, g_ref, wr_ref, x1_ref, h_ref, route_ref):
    x1 = _mix_out(x_ref, a_ref, r_ref, wo_ref)
    x1_ref[...] = x1
    h = _rms(x1, g_ref[...])
    h_ref[...] = h
    logits = jnp.dot(h, wr_ref[...], preferred_element_type=F32, precision=lax.Precision.HIGHEST)
    lane = lax.broadcasted_iota(jnp.int32, logits.shape, 1)
    neg = -jnp.inf
    lg = jnp.where(lane < N_EXPERTS, logits, neg)
    m1 = jnp.max(lg, axis=-1, keepdims=True)
    i1 = jnp.min(jnp.where(lg == m1, lane, LANES), axis=-1, keepdims=True)
    lg2 = jnp.where(lane == i1, neg, lg)
    m2 = jnp.max(lg2, axis=-1, keepdims=True)
    i2 = jnp.min(jnp.where(lg2 == m2, lane, LANES), axis=-1, keepdims=True)
    e2 = jnp.exp(m2 - m1)
    g1 = 1.0 / (1.0 + e2)
    g2 = e2 / (1.0 + e2)
    route = jnp.where(lane == 0, i1.astype(F32),
                      jnp.where(lane == 1, i2.astype(F32),
                                jnp.where(lane == 2, g1, jnp.where(lane == 3, g2, 0.0))))
    route_ref[...] = route


def _router(x2, a, r, w_out, gain, w_router):
    T, D = x2.shape
    tm = min(TOK_TILE, T)
    tok = lambda i: (i, 0)
    return pl.pallas_call(
        _router_body,
        grid=(T // tm,),
        in_specs=[
            pl.BlockSpec((tm, D), tok),
            pl.BlockSpec((tm, MLA_WIDTH), tok),
            pl.BlockSpec((tm, HG_WIDTH), tok),
            _resident(w_out.shape),
            _resident((1, D)),
            _resident((D, LANES)),
        ],
        out_specs=[pl.BlockSpec((tm, D), tok), pl.BlockSpec((tm, D), tok), pl.BlockSpec((tm, LANES), tok)],
        out_shape=[jax.ShapeDtypeStruct((T, D), F32), jax.ShapeDtypeStruct((T, D), F32),
                   jax.ShapeDtypeStruct((T, LANES), F32)],
        compiler_params=_cparams(("parallel",)),
        name="mix_out_router",
    )(x2, a, r, w_out, gain, w_router)


def _dispatch_body(pos_ref, fill_ref, h_ref, xs_ref, zero_ref, sem, *, tm, tile):
    i = pl.program_id(0)

    @pl.when(i == 0)
    def _():
        zero_ref[...] = jnp.zeros_like(zero_ref)
        for e in range(2 * N_EXPERTS):
            @pl.when(fill_ref[e] >= 0)
            def _():
                cp = pltpu.make_async_copy(zero_ref, xs_ref.at[pl.ds(pl.multiple_of(fill_ref[e], tile), tile), :], sem)
                cp.start()
                cp.wait()

    def row_copy(t, k):
        dst = pos_ref[(i * tm + t) * TOP_K + k]
        return pltpu.make_async_copy(h_ref.at[pl.ds(t, 1), :], xs_ref.at[pl.ds(dst, 1), :], sem)

    def issue(t, c):
        for k in range(TOP_K):
            row_copy(t, k).start()
        return c

    def drain(t, c):
        for k in range(TOP_K):
            row_copy(t, k).wait()
        return c

    lax.fori_loop(0, tm, issue, 0)
    lax.fori_loop(0, tm, drain, 0)


def _dispatch(pos_flat, fill_start, h, n_rows, tile):
    T, D = h.shape
    tm = min(ROW_TILE, T)
    grid_spec = pltpu.PrefetchScalarGridSpec(
        num_scalar_prefetch=2,
        grid=(T // tm,),
        in_specs=[pl.BlockSpec((tm, D), lambda i, *_: (i, 0))],
        out_specs=pl.BlockSpec(memory_space=pl.ANY),
        scratch_shapes=[pltpu.VMEM((tile, D), F32), pltpu.SemaphoreType.DMA(())],
    )
    return pl.pallas_call(
        functools.partial(_dispatch_body, tm=tm, tile=tile),
        grid_spec=grid_spec,
        out_shape=jax.ShapeDtypeStruct((n_rows, D), F32),
        compiler_params=_cparams(("arbitrary",)),
        name="moe_dispatch",
    )(pos_flat, fill_start, h)


def _expert_body(te_ref, nv_ref, xs_ref, wg_ref, wu_ref, wd_ref, ys_ref, acc_ref):
    t = pl.program_id(0)
    j = pl.program_id(1)

    @pl.when(t < nv_ref[0])
    def _():
        x = xs_ref[...].astype(BF16)
        gate = jnp.dot(x, wg_ref[0], preferred_element_type=F32)
        up = jnp.dot(x, wu_ref[0], preferred_element_type=F32)
        act = (gate * _sigmoid(gate) * up).astype(BF16)
        y = jnp.dot(act, wd_ref[0], preferred_element_type=F32)

        @pl.when(j == 0)
        def _():
            acc_ref[...] = y

        @pl.when(j > 0)
        def _():
            acc_ref[...] += y

        @pl.when(j == pl.num_programs(1) - 1)
        def _():
            ys_ref[...] = acc_ref[...]

    @pl.when(t >= nv_ref[0])
    def _():
        ys_ref[...] = jnp.zeros_like(ys_ref)


def _experts(tile_expert, n_valid, xs, wg, wu, wd, tile):
    P, D = xs.shape
    ff = wg.shape[2]
    fs = ff // MOE_FF_SPLIT
    n_tiles = P // tile

    def row_map(t, j, te, nv):
        return (jnp.minimum(t, nv[0] - 1), 0)

    grid_spec = pltpu.PrefetchScalarGridSpec(
        num_scalar_prefetch=2,
        grid=(n_tiles, MOE_FF_SPLIT),
        in_specs=[
            pl.BlockSpec((tile, D), row_map),
            pl.BlockSpec((1, D, fs), lambda t, j, te, nv: (te[t], 0, j)),
            pl.BlockSpec((1, D, fs), lambda t, j, te, nv: (te[t], 0, j)),
            pl.BlockSpec((1, fs, D), lambda t, j, te, nv: (te[t], j, 0)),
        ],
        out_specs=pl.BlockSpec((tile, D), lambda t, j, te, nv: (t, 0)),
        scratch_shapes=[pltpu.VMEM((tile, D), F32)],
    )
    return pl.pallas_call(
        _expert_body,
        grid_spec=grid_spec,
        out_shape=jax.ShapeDtypeStruct((P, D), F32),
        compiler_params=_cparams(("arbitrary", "arbitrary")),
        name="moe_experts",
    )(tile_expert, n_valid, xs, wg, wu, wd)


def _combine_body(pos_ref, x1_ref, route_ref, ys_ref, *rest, tm, final):
    if final:
        fg_ref, o_ref, y0_ref, y1_ref, sem = rest
    else:
        o_ref, y0_ref, y1_ref, sem = rest
    i = pl.program_id(0)
    bufs = (y0_ref, y1_ref)

    def row_copy(t, k):
        src = pos_ref[(i * tm + t) * TOP_K + k]
        return pltpu.make_async_copy(ys_ref.at[pl.ds(src, 1), :], bufs[k].at[pl.ds(t, 1), :], sem)

    def issue(t, c):
        for k in range(TOP_K):
            row_copy(t, k).start()
        return c

    def drain(t, c):
        for k in range(TOP_K):
            row_copy(t, k).wait()
        return c

    lax.fori_loop(0, tm, issue, 0)
    lax.fori_loop(0, tm, drain, 0)
    route = route_ref[...]
    y = x1_ref[...] + route[:, 2:3] * y0_ref[...] + route[:, 3:4] * y1_ref[...]
    if final:
        y = _rms(y, fg_ref[...])
    o_ref[...] = y


def _combine(pos_flat, x1, route, ys, final_gain):
    T, D = x1.shape
    tm = min(ROW_TILE, T)
    final = final_gain is not None
    tok = lambda i, *_: (i, 0)
    in_specs = [
        pl.BlockSpec((tm, D), tok),
        pl.BlockSpec((tm, LANES), tok),
        pl.BlockSpec(memory_space=pl.ANY),
    ]
    args = [pos_flat, x1, route, ys]
    if final:
        in_specs.append(pl.BlockSpec((1, D), lambda i, *_: (0, 0)))
        args.append(final_gain)
    grid_spec = pltpu.PrefetchScalarGridSpec(
        num_scalar_prefetch=1,
        grid=(T // tm,),
        in_specs=in_specs,
        out_specs=pl.BlockSpec((tm, D), tok),
        scratch_shapes=[pltpu.VMEM((tm, D), F32), pltpu.VMEM((tm, D), F32), pltpu.SemaphoreType.DMA(())],
    )
    return pl.pallas_call(
        functools.partial(_combine_body, tm=tm, final=final),
        grid_spec=grid_spec,
        out_shape=jax.ShapeDtypeStruct((T, D), F32),
        compiler_params=_cparams(("arbitrary",)),
        name="moe_combine",
    )(*args)


def _moe_layer(x2, a, r, w_out, gain, w_router, wg, wu, wd, final_gain):
    T, D = x2.shape
    tile = min(MOE_TM, T)
    wr = jnp.zeros((D, LANES), F32).at[:, :N_EXPERTS].set(w_router)
    x1, h, route = _router(x2, a, r, w_out, gain, wr)

    n_slots = T * TOP_K
    n_rows = n_slots + N_EXPERTS * tile
    flat_e = route[:, :TOP_K].astype(jnp.int32).reshape(-1)
    onehot = (flat_e[:, None] == jnp.arange(N_EXPERTS, dtype=jnp.int32)[None, :]).astype(jnp.int32)
    csum = jnp.cumsum(onehot, axis=0)
    rank = jnp.sum(onehot * (csum - onehot), axis=1)
    counts = csum[-1]
    padded = ((counts + tile - 1) // tile) * tile
    ends = jnp.cumsum(padded)
    starts = ends - padded
    pos_flat = (jnp.sum(onehot * starts[None, :], axis=1) + rank).astype(jnp.int32)
    tail_row = n_rows - tile * jnp.arange(1, N_EXPERTS + 1, dtype=jnp.int32)
    fill_start = jnp.concatenate([jnp.where(padded > 0, ends - tile, -1),
                                  jnp.where(tail_row >= ends[-1], tail_row, -1)]).astype(jnp.int32)
    tile_row = jnp.arange(n_rows // tile, dtype=jnp.int32) * tile
    tile_expert = jnp.minimum(jnp.sum(tile_row[:, None] >= ends[None, :], axis=1), N_EXPERTS - 1).astype(jnp.int32)
    n_valid = (ends[-1:] // tile).astype(jnp.int32)

    xs = _dispatch(pos_flat, fill_start, h, n_rows, tile)
    ys = _experts(tile_expert, n_valid, xs, wg, wu, wd, tile)
    return _combine(pos_flat, x1, route, ys, final_gain)


def _rotate_half_cols(w):
    half = QK_ROPE // 2
    return jnp.concatenate([-w[..., half:], w[..., :half]], axis=-1)


def _prep_w_in(w):
    o = Q_RANK + KV_RANK
    k_pe = w[:, o:o + QK_ROPE]
    return jnp.concatenate([w[:, :o], w[:, o + QK_ROPE:], k_pe, _rotate_half_cols(k_pe)], axis=1).astype(BF16)


def _prep_w_q(w):
    w3 = w.reshape(Q_RANK, MLA_HEADS, QK_DIM)
    nope = w3[:, :, :QK_NOPE].reshape(Q_RANK, MLA_HEADS * QK_NOPE)
    pe = w3[:, :, QK_NOPE:]
    pair = jnp.concatenate([pe, _rotate_half_cols(pe)], axis=-1).reshape(Q_RANK, MLA_HEADS * 2 * QK_ROPE)
    return jnp.concatenate([nope, pair], axis=1).astype(BF16)


def _prep_w_kv(w):
    w4 = w.reshape(KV_RANK, MLA_HEADS, 2, QK_NOPE)
    return w4.transpose(0, 2, 1, 3).reshape(KV_RANK, 2 * MLA_HEADS * QK_NOPE).astype(BF16)


def kernel(x, positions, mix_norm, w_in, q_a_norm, w_q_b, kv_a_norm, w_kv_b, hg_lower_bounds, hg_out_norm, w_out, ffn_norm, dense_w_gate, dense_w_up, dense_w_down, moe_router, moe_w_gate, moe_w_up, moe_w_down, final_norm):
    B, S, D = x.shape
    depth = w_in.shape[0]
    lb_p = jax.nn.softmax(hg_lower_bounds.astype(F32), axis=0)
    lower = jnp.cumsum(lb_p, axis=0) - lb_p[0:1]
    inv = ROPE_THETA ** (-jnp.arange(0, QK_ROPE, 2, dtype=F32) / QK_ROPE)
    invf = jnp.tile(inv, 2 * QK_ROPE // inv.shape[0]).reshape(1, LANES)
    pos3 = positions.reshape(B, S, 1)
    final_gain = final_norm.reshape(1, D)

    x2 = x.reshape(B * S, D)
    for l in range(depth):
        q, k, v, hqig, hf = _mix_in(
            x2.reshape(B, S, D), pos3, mix_norm[l].reshape(1, D), _prep_w_in(w_in[l]),
            q_a_norm[l].reshape(1, Q_RANK), _prep_w_q(w_q_b[l]),
            kv_a_norm[l].reshape(1, KV_RANK), _prep_w_kv(w_kv_b[l]), invf)
        a = _attention(q, k, v).reshape(B * S, MLA_WIDTH)
        r = _hgrn(hqig, hf, lower[l].reshape(1, HG_FDIM), hg_out_norm[l].reshape(1, HG_VAL), B, S)
        fg = final_gain if l == depth - 1 else None
        wo = w_out[l].astype(BF16)
        gain = ffn_norm[l].reshape(1, D)
        j = l // 2
        if l % 2 == 0:
            x2 = _dense_layer(x2, a, r, wo, gain, dense_w_gate[j].astype(BF16), dense_w_up[j].astype(BF16),
                              dense_w_down[j].astype(BF16), fg)
        else:
            x2 = _moe_layer(x2, a, r, wo, gain, moe_router[j], moe_w_gate[j].astype(BF16),
                            moe_w_up[j].astype(BF16), moe_w_down[j].astype(BF16), fg)
    return x2.reshape(B, S, D)
```

```python
import functools

import numpy as np
import jax
import jax.numpy as jnp
from jax import lax
from jax.experimental import pallas as pl
from jax.experimental.pallas import tpu as pltpu

F32 = jnp.float32
BF16 = jnp.bfloat16

D_MODEL = 1024
MLA_HEADS = 4
QK_NOPE = 128
QK_ROPE = 64
QK_DIM = QK_NOPE + QK_ROPE
V_HEAD = 128
Q_RANK = 384
KV_RANK = 256
ROPE_THETA = 10000.0
HG_HEADS = 4
HG_KEY = 128
HG_VAL = 128
HG_CHUNK = 64
HG_SUB = 8
HG_FDIM = HG_HEADS * HG_KEY
HG_WIDTH = HG_HEADS * HG_VAL
MLA_WIDTH = MLA_HEADS * V_HEAD
N_EXPERTS = 8
TOP_K = 2
EPS = 1e-6
F_FLOOR = 1e-30
NEG_BIG = float(np.finfo(np.float32).min)
LOG2_E = float(np.log2(np.e))

LANES = 128
VMEM_LIMIT = 56 * 1024 * 1024

TOK_TILE = 512
ATTN_HEADS_PER_STEP = 2
HG_BLOCK = 512
MOE_TM = 512
MOE_FF_SPLIT = 2
GROUP_ALIGN = 8
GROUP_CHUNKS = tuple(GROUP_ALIGN << p for p in reversed(range((TOK_TILE // GROUP_ALIGN).bit_length())))


def _cparams(sem):
    return pltpu.CompilerParams(dimension_semantics=sem, vmem_limit_bytes=VMEM_LIMIT)


def _rms(x, g):
    ms = jnp.mean(x * x, axis=-1, keepdims=True)
    return x * lax.rsqrt(ms + EPS) * g


def _sigmoid(x):
    return 1.0 / (1.0 + jnp.exp(-x))


def _resident(shape):
    nd = len(shape)
    return pl.BlockSpec(shape, lambda *_: (0,) * nd, pipeline_mode=pl.Buffered(1))


def _mix_in_body(x_ref, pos_ref, g_ref, win_ref, qg_ref, wq_ref, kvg_ref, wkv_ref, invf_ref,
                 q_ref, k_ref, v_ref, hqig_ref, hf_ref):
    x = x_ref[0]
    h = _rms(x, g_ref[...]).astype(BF16)

    def proj(a, b):
        return jnp.dot(h, win_ref[:, a:b], preferred_element_type=F32)

    ang = pos_ref[0].astype(F32) * invf_ref[...]
    lane = lax.broadcasted_iota(jnp.int32, ang.shape, 1)
    cs = jnp.where(lane < QK_ROPE, jnp.cos(ang), jnp.sin(ang))

    def rope(pair):
        p = pair * cs
        return p[:, :QK_ROPE] + p[:, QK_ROPE:]

    scale = QK_DIM ** -0.5 * LOG2_E
    o = 0
    c_q = proj(o, o + Q_RANK); o += Q_RANK
    c_kv = proj(o, o + KV_RANK); o += KV_RANK
    hqig_ref[:, 0:HG_FDIM] = proj(o, o + HG_FDIM).astype(BF16); o += HG_FDIM
    hf_ref[...] = proj(o, o + HG_FDIM); o += HG_FDIM
    hqig_ref[:, HG_FDIM:HG_FDIM + HG_WIDTH] = proj(o, o + HG_WIDTH).astype(BF16); o += HG_WIDTH
    hqig_ref[:, HG_FDIM + HG_WIDTH:] = proj(o, o + HG_WIDTH).astype(BF16); o += HG_WIDTH
    k_pe = rope(proj(o, o + 2 * QK_ROPE)).astype(BF16)

    tn = (((0,), (1,)), ((), ()))
    nw = MLA_HEADS * QK_NOPE
    qn = _rms(c_q, qg_ref[...]).astype(BF16)
    kvn = _rms(c_kv, kvg_ref[...]).astype(BF16)
    qf_t = lax.dot_general(wq_ref[...], qn, tn, preferred_element_type=F32)
    kf = jnp.dot(kvn, wkv_ref[:, 0:nw], preferred_element_type=F32)
    vf_t = lax.dot_general(wkv_ref[:, nw:], kvn, tn, preferred_element_type=F32)
    cs_t = cs.T
    for hd in range(MLA_HEADS):
        q_ref[0, hd, 0:QK_NOPE, :] = (qf_t[hd * QK_NOPE:(hd + 1) * QK_NOPE, :] * scale).astype(BF16)
        pr = qf_t[nw + hd * 2 * QK_ROPE: nw + (hd + 1) * 2 * QK_ROPE, :] * cs_t
        q_ref[0, hd, QK_NOPE:QK_DIM, :] = ((pr[:QK_ROPE, :] + pr[QK_ROPE:, :]) * scale).astype(BF16)
        k_ref[0, hd, :, 0:QK_NOPE] = kf[:, hd * QK_NOPE:(hd + 1) * QK_NOPE].astype(BF16)
        k_ref[0, hd, :, QK_NOPE:QK_DIM] = k_pe
        v_ref[0, hd, 0] = vf_t[hd * V_HEAD:(hd + 1) * V_HEAD, :].astype(BF16)


def _mix_in(x, pos3, gain, w_in, q_gain, w_q, kv_gain, w_kv, invf):
    B, S, D = x.shape
    tm = min(TOK_TILE, S)
    ns = S // tm
    T = B * S
    ncol = w_in.shape[1]
    tok = lambda b, i: (b * ns + i, 0)
    return pl.pallas_call(
        _mix_in_body,
        grid=(B, ns),
        in_specs=[
            pl.BlockSpec((1, tm, D), lambda b, i: (b, i, 0)),
            pl.BlockSpec((1, tm, 1), lambda b, i: (b, i, 0)),
            _resident((1, D)),
            _resident((D, ncol)),
            _resident((1, Q_RANK)),
            _resident(w_q.shape),
            _resident((1, KV_RANK)),
            _resident(w_kv.shape),
            _resident((1, LANES)),
        ],
        out_specs=[
            pl.BlockSpec((1, MLA_HEADS, QK_DIM, tm), lambda b, i: (b, 0, 0, i)),
            pl.BlockSpec((1, MLA_HEADS, tm, QK_DIM), lambda b, i: (b, 0, i, 0)),
            pl.BlockSpec((1, MLA_HEADS, 1, V_HEAD, tm), lambda b, i: (b, 0, i, 0, 0)),
            pl.BlockSpec((tm, HG_FDIM + 2 * HG_WIDTH), tok),
            pl.BlockSpec((tm, HG_FDIM), tok),
        ],
        out_shape=[
            jax.ShapeDtypeStruct((B, MLA_HEADS, QK_DIM, S), BF16),
            jax.ShapeDtypeStruct((B, MLA_HEADS, S, QK_DIM), BF16),
            jax.ShapeDtypeStruct((B, MLA_HEADS, ns, V_HEAD, tm), BF16),
            jax.ShapeDtypeStruct((T, HG_FDIM + 2 * HG_WIDTH), BF16),
            jax.ShapeDtypeStruct((T, HG_FDIM), F32),
        ],
        compiler_params=_cparams(("parallel", "parallel")),
        name="mix_in",
    )(x, pos3, gain, w_in, q_gain, w_q, kv_gain, w_kv, invf)


def _attn_body(q_ref, k_ref, v_ref, o_ref, s0_ref, s1_ref, m_ref, l_ref, acc_ref, *, tq, tk, nh):
    qi = pl.program_id(2)

    heads = range(nh)

    def scores(kb, s_ref):
        start = pl.multiple_of(kb * tk, tk)
        for hd in heads:
            s_ref[hd] = jnp.dot(k_ref[0, hd, pl.ds(start, tk), :], q_ref[0, hd],
                                preferred_element_type=F32)

    def update(kb, s_of_head):
        for hd in heads:
            s = s_of_head(hd)
            m = m_ref[hd]
            m_new = jnp.maximum(m, jnp.max(s, axis=0, keepdims=True))
            alpha = jnp.exp2(m - m_new)
            p = jnp.exp2(s - m_new)
            m_ref[hd] = m_new
            l_ref[hd] = alpha * l_ref[hd] + jnp.sum(p, axis=0, keepdims=True)
            acc_ref[hd] = alpha * acc_ref[hd] + jnp.dot(v_ref[0, hd, kb], p.astype(BF16),
                                                        preferred_element_type=F32)

    def diagonal_scores(hd):
        start = pl.multiple_of(qi * tk, tk)
        s = jnp.dot(k_ref[0, hd, pl.ds(start, tk), :], q_ref[0, hd], preferred_element_type=F32)
        key = lax.broadcasted_iota(jnp.int32, s.shape, 0)
        qry = lax.broadcasted_iota(jnp.int32, s.shape, 1)
        return jnp.where(key <= qry, s, NEG_BIG)

    m_ref[...] = jnp.full(m_ref.shape, NEG_BIG, F32)
    l_ref[...] = jnp.zeros(l_ref.shape, F32)
    acc_ref[...] = jnp.zeros(acc_ref.shape, F32)
    scores(0, s0_ref)
    update(qi, diagonal_scores)

    def pair(j, carry):
        scores(2 * j + 1, s1_ref)
        update(2 * j, lambda hd: s0_ref[hd])
        scores(2 * j + 2, s0_ref)
        update(2 * j + 1, lambda hd: s1_ref[hd])
        return carry

    lax.fori_loop(0, qi // 2, pair, 0)

    @pl.when(qi % 2 == 1)
    def _():
        update(qi - 1, lambda hd: s0_ref[hd])

    for hd in heads:
        o_ref[0, :, hd * V_HEAD:(hd + 1) * V_HEAD] = (acc_ref[hd] / l_ref[hd]).T.astype(o_ref.dtype)


def _attention(q, k, v):
    B, H, S, _ = k.shape
    tq = v.shape[-1]
    nh = ATTN_HEADS_PER_STEP
    return pl.pallas_call(
        functools.partial(_attn_body, tq=tq, tk=tq, nh=nh),
        grid=(B, H // nh, S // tq),
        in_specs=[
            pl.BlockSpec((1, nh, QK_DIM, tq), lambda b, h, i: (b, h, 0, i)),
            pl.BlockSpec((1, nh, S, QK_DIM), lambda b, h, i: (b, h, 0, 0)),
            pl.BlockSpec((1, nh, S // tq, V_HEAD, tq), lambda b, h, i: (b, h, 0, 0, 0)),
        ],
        out_specs=pl.BlockSpec((1, tq, nh * V_HEAD), lambda b, h, i: (b, i, h)),
        out_shape=jax.ShapeDtypeStruct((B, S, H * V_HEAD), BF16),
        scratch_shapes=[
            pltpu.VMEM((nh, tq, tq), F32),
            pltpu.VMEM((nh, tq, tq), F32),
            pltpu.VMEM((nh, 1, tq), F32),
            pltpu.VMEM((nh, 1, tq), F32),
            pltpu.VMEM((nh, V_HEAD, tq), F32),
        ],
        compiler_params=_cparams(("parallel", "parallel", "arbitrary")),
        name="mla_attention",
    )(q, k, v)


def _hgrn_body(hqig_ref, hf_ref, lb_ref, gain_ref, r_ref, state_ref, *, n_chunks):
    C = HG_CHUNK

    @pl.when(pl.program_id(1) == 0)
    def _():
        state_ref[...] = jnp.zeros_like(state_ref)

    row = lax.broadcasted_iota(jnp.int32, (C, C), 0)
    col = lax.broadcasted_iota(jnp.int32, (C, C), 1)
    tri = jnp.where(col <= row, 1.0, 0.0).astype(BF16)
    lb = lb_ref[...]
    gain = gain_ref[...]

    levels = []
    L = C // 2
    while L >= HG_SUB:
        levels.append(L)
        L //= 2
    level_masks = [((row // L) % 2 == 1) & ((col // L) % 2 == 0) & (row // (2 * L) == col // (2 * L))
                   for L in levels]
    frow = lax.broadcasted_iota(jnp.int32, (C, HG_FDIM), 0)
    later_half = [(frow // L) % 2 == 1 for L in levels]
    diag_mask = (col // HG_SUB == row // HG_SUB) & (col % HG_SUB <= row % HG_SUB)
    src = lax.broadcasted_iota(jnp.int32, (HG_SUB * HG_KEY, C), 0) // HG_KEY
    dst = lax.broadcasted_iota(jnp.int32, (HG_SUB * HG_KEY, C), 1) % HG_SUB
    spread = jnp.where(src == dst, 1.0, 0.0).astype(BF16)

    def chunk(c, carry):
        r0 = pl.multiple_of(c * C, C)
        rows = pl.ds(r0, C)
        q_raw = hqig_ref[rows, 0:HG_FDIM].astype(F32)
        z = hf_ref[rows, :]
        q = q_raw * _sigmoid(q_raw)
        sig = _sigmoid(z)
        f = lb + (1.0 - lb) * sig
        log_f = jnp.log2(jnp.maximum(f, F_FLOOR))
        k = (1.0 - lb) * (1.0 - sig)

        p0 = log_f.astype(BF16)
        e0 = log_f - p0.astype(F32)
        p1 = e0.astype(BF16)
        p2 = (e0 - p1.astype(F32)).astype(BF16)
        b = (jnp.dot(tri, p0, preferred_element_type=F32)
             + jnp.dot(tri, p1, preferred_element_type=F32)
             + jnp.dot(tri, p2, preferred_element_type=F32))
        b_last = b[C - 1:C, :]
        q_dec = (q * jnp.exp2(b)).astype(BF16)
        k_dec = (k * jnp.exp2(b_last - b)).astype(BF16)
        s_dec = jnp.exp2(b_last)

        nsub = C // HG_SUB
        b3 = b.reshape(nsub, HG_SUB, HG_FDIM)
        q3 = q.reshape(nsub, HG_SUB, HG_FDIM)
        k3 = k.reshape(nsub, HG_SUB, HG_FDIM)
        diag = []
        for s in range(HG_SUB):
            e = jnp.exp2(jnp.minimum(b3 - b3[:, s:s + 1, :], 0.0))
            diag.append((q3 * e * k3[:, s:s + 1, :]).reshape(C, HG_FDIM).astype(BF16))

        lvl = []
        for L, later in zip(levels, later_half):
            g = C // (2 * L)
            bg = b.reshape(g, 2 * L, HG_FDIM)
            ref = jnp.broadcast_to(bg[:, L - 1:L, :], bg.shape).reshape(C, HG_FDIM)
            lvl.append((jnp.where(later, q, k) * jnp.exp2(-jnp.abs(b - ref))).astype(BF16))

        for hd in range(HG_HEADS):
            lo = hd * HG_KEY
            hs = slice(lo, lo + HG_KEY)
            dp = jnp.concatenate([p[:, hs] for p in diag], axis=1)
            a = jnp.where(diag_mask, jnp.dot(dp, spread, preferred_element_type=F32), 0.0)
            for i in range(len(levels)):
                mh = lvl[i][:, hs]
                sc = lax.dot_general(mh, mh, (((1,), (1,)), ((), ())), preferred_element_type=F32)
                a = a + jnp.where(level_masks[i], sc, 0.0)
            v = hqig_ref[rows, HG_FDIM + hd * HG_VAL: HG_FDIM + (hd + 1) * HG_VAL]
            st = state_ref[hd]
            o = jnp.dot(a.astype(BF16), v, preferred_element_type=F32)
            o = o + lax.dot_general(q_dec[:, hs], st.astype(BF16), (((1,), (1,)), ((), ())),
                                    preferred_element_type=F32)
            state_ref[hd] = s_dec[:, hs] * st + lax.dot_general(
                v, k_dec[:, hs], (((0,), (0,)), ((), ())), preferred_element_type=F32)
            g_raw = hqig_ref[rows, HG_FDIM + HG_WIDTH + hd * HG_VAL:
                             HG_FDIM + HG_WIDTH + (hd + 1) * HG_VAL].astype(F32)
            y = _rms(o, gain) * (g_raw * _sigmoid(g_raw))
            r_ref[rows, hd * HG_VAL:(hd + 1) * HG_VAL] = y.astype(r_ref.dtype)
        return carry

    lax.fori_loop(0, n_chunks, chunk, 0, unroll=2)


def _hgrn(hqig, hf, lower, out_gain, B, S):
    tb = min(HG_BLOCK, S)
    nb = S // tb
    T = B * S
    tok = lambda b, i: (b * nb + i, 0)
    return pl.pallas_call(
        functools.partial(_hgrn_body, n_chunks=tb // HG_CHUNK),
        grid=(B, nb),
        in_specs=[
            pl.BlockSpec((tb, HG_FDIM + 2 * HG_WIDTH), tok),
            pl.BlockSpec((tb, HG_FDIM), tok),
            _resident((1, HG_FDIM)),
            _resident((1, HG_VAL)),
        ],
        out_specs=pl.BlockSpec((tb, HG_WIDTH), tok),
        out_shape=jax.ShapeDtypeStruct((T, HG_WIDTH), BF16),
        scratch_shapes=[pltpu.VMEM((HG_HEADS, HG_VAL, HG_KEY), F32)],
        compiler_params=_cparams(("parallel", "arbitrary")),
        name="hgrn2_scan",
    )(hqig, hf, lower, out_gain)


def _mix_out(x_ref, a_ref, r_ref, wo_ref):
    y = jnp.dot(a_ref[...], wo_ref[0:MLA_WIDTH, :], preferred_element_type=F32)
    y = y + jnp.dot(r_ref[...], wo_ref[MLA_WIDTH:, :], preferred_element_type=F32)
    return x_ref[...] + y


def _dense_body(x_ref, a_ref, r_ref, wo_ref, g_ref, wg_ref, wu_ref, wd_ref, *rest, n_split, final):
    if final:
        fg_ref, o_ref = rest
    else:
        (o_ref,) = rest
    x1 = _mix_out(x_ref, a_ref, r_ref, wo_ref)
    h = _rms(x1, g_ref[...]).astype(BF16)
    ff = wg_ref.shape[1]
    step = ff // n_split
    y = x1
    for j in range(n_split):
        cs = slice(j * step, (j + 1) * step)
        gate = jnp.dot(h, wg_ref[:, cs], preferred_element_type=F32)
        up = jnp.dot(h, wu_ref[:, cs], preferred_element_type=F32)
        act = (gate * _sigmoid(gate) * up).astype(BF16)
        y = y + jnp.dot(act, wd_ref[cs, :], preferred_element_type=F32)
    if final:
        y = _rms(y, fg_ref[...])
    o_ref[...] = y


def _dense_layer(x2, a, r, w_out, gain, wg, wu, wd, final_gain):
    T, D = x2.shape
    tm = min(TOK_TILE, T)
    ff = wg.shape[1]
    tok = lambda i: (i, 0)
    final = final_gain is not None
    in_specs = [
        pl.BlockSpec((tm, D), tok),
        pl.BlockSpec((tm, MLA_WIDTH), tok),
        pl.BlockSpec((tm, HG_WIDTH), tok),
        _resident(w_out.shape),
        _resident((1, D)),
        _resident((D, ff)),
        _resident((D, ff)),
        _resident((ff, D)),
    ]
    args = [x2, a, r, w_out, gain, wg, wu, wd]
    if final:
        in_specs.append(_resident((1, D)))
        args.append(final_gain)
    return pl.pallas_call(
        functools.partial(_dense_body, n_split=2, final=final),
        grid=(T // tm,),
        in_specs=in_specs,
        out_specs=pl.BlockSpec((tm, D), tok),
        out_shape=jax.ShapeDtypeStruct((T, D), F32),
        compiler_params=_cparams(("parallel",)),
        name="mix_out_dense_ffn",
    )(*args)


def _router_body(x_ref, a_ref, r_ref, wo_ref, g_ref, wr_ref, x1_ref, h_ref, route_ref):
    x1 = _mix_out(x_ref, a_ref, r_ref, wo_ref)
    x1_ref[...] = x1
    h = _rms(x1, g_ref[...])
    h_ref[...] = h.astype(h_ref.dtype)
    logits = jnp.dot(h, wr_ref[...], preferred_element_type=F32, precision=lax.Precision.HIGHEST)
    lane = lax.broadcasted_iota(jnp.int32, logits.shape, 1)
    neg = -jnp.inf
    lg = jnp.where(lane < N_EXPERTS, logits, neg)
    m1 = jnp.max(lg, axis=-1, keepdims=True)
    i1 = jnp.min(jnp.where(lg == m1, lane, LANES), axis=-1, keepdims=True)
    lg2 = jnp.where(lane == i1, neg, lg)
    m2 = jnp.max(lg2, axis=-1, keepdims=True)
    i2 = jnp.min(jnp.where(lg2 == m2, lane, LANES), axis=-1, keepdims=True)
    e2 = jnp.exp(m2 - m1)
    g1 = 1.0 / (1.0 + e2)
    g2 = e2 / (1.0 + e2)
    route = jnp.where(lane == 0, i1.astype(F32),
                      jnp.where(lane == 1, i2.astype(F32),
                                jnp.where(lane == 2, g1, jnp.where(lane == 3, g2, 0.0))))
    route_ref[...] = route


def _router(x2, a, r, w_out, gain, w_router):
    T, D = x2.shape
    tm = min(TOK_TILE, T)
    tok = lambda i: (i, 0)
    return pl.pallas_call(
        _router_body,
        grid=(T // tm,),
        in_specs=[
            pl.BlockSpec((tm, D), tok),
            pl.BlockSpec((tm, MLA_WIDTH), tok),
            pl.BlockSpec((tm, HG_WIDTH), tok),
            _resident(w_out.shape),
            _resident((1, D)),
            _resident((D, LANES)),
        ],
        out_specs=[pl.BlockSpec((tm, D), tok), pl.BlockSpec((tm, D), tok), pl.BlockSpec((tm, LANES), tok)],
        out_shape=[jax.ShapeDtypeStruct((T, D), F32), jax.ShapeDtypeStruct((T, D), BF16),
                   jax.ShapeDtypeStruct((T, LANES), F32)],
        compiler_params=_cparams(("parallel",)),
        name="mix_out_router",
    )(x2, a, r, w_out, gain, w_router)


def _group_copies(count, src_ref, src_off, dst_ref, dst_off, sem, act):
    done = 0
    for p in GROUP_CHUNKS:
        take = (count & p) != 0

        @pl.when(take)
        def _():
            src = src_ref.at[pl.ds(pl.multiple_of(src_off + done, GROUP_ALIGN), p), :]
            dst = dst_ref.at[pl.ds(pl.multiple_of(dst_off + done, GROUP_ALIGN), p), :]
            act(pltpu.make_async_copy(src, dst, sem))

        done = done + jnp.where(take, p, 0)


def _dispatch_body(cnt_ref, loc_ref, glb_ref, fill_ref, info_ref, h_ref, xs_ref,
                   sorted_ref, zero_ref, sem, fill_sem, *, tile):
    i = pl.program_id(0)
    n = pl.num_programs(0)
    slot = i % 2

    @pl.when(i == 0)
    def _():
        zero_ref[...] = jnp.zeros_like(zero_ref)
        for e in range(fill_ref.shape[0]):
            @pl.when(fill_ref[e] >= 0)
            def _():
                cp = pltpu.make_async_copy(
                    zero_ref, xs_ref.at[pl.ds(pl.multiple_of(fill_ref[e], tile), tile), :], fill_sem)
                cp.start()
                cp.wait()

    info = info_ref[...]
    rows = sorted_ref.shape[1]
    lane = lax.broadcasted_iota(jnp.int32, (info.shape[0], rows), 1).astype(F32)
    onehot = jnp.where((lane == info[:, 0:1]) | (lane == info[:, 1:2]), 1.0, 0.0).astype(BF16)
    sorted_ref[slot] = lax.dot_general(onehot, h_ref[...], (((0,), (0,)), ((), ())),
                                       preferred_element_type=F32)

    def copies(step, buf, act):
        for e in range(N_EXPERTS):
            j = step * N_EXPERTS + e
            _group_copies(cnt_ref[j], sorted_ref.at[buf], loc_ref[j], xs_ref, glb_ref[j], sem.at[buf], act)

    copies(i, slot, lambda cp: cp.start())

    @pl.when(i > 0)
    def _():
        copies(i - 1, 1 - slot, lambda cp: cp.wait())

    @pl.when(i == n - 1)
    def _():
        copies(i, slot, lambda cp: cp.wait())


def _dispatch(cnt, loc, glb, fill_start, info, h, n_rows, tile, local_rows):
    T, D = h.shape
    tm = min(TOK_TILE, T)
    tok = lambda i, *_: (i, 0)
    grid_spec = pltpu.PrefetchScalarGridSpec(
        num_scalar_prefetch=4,
        grid=(T // tm,),
        in_specs=[pl.BlockSpec((tm, LANES), tok), pl.BlockSpec((tm, D), tok)],
        out_specs=pl.BlockSpec(memory_space=pl.ANY),
        scratch_shapes=[pltpu.VMEM((2, local_rows, D), F32), pltpu.VMEM((tile, D), F32),
                        pltpu.SemaphoreType.DMA((2,)), pltpu.SemaphoreType.DMA(())],
    )
    return pl.pallas_call(
        functools.partial(_dispatch_body, tile=tile),
        grid_spec=grid_spec,
        out_shape=jax.ShapeDtypeStruct((n_rows, D), F32),
        compiler_params=_cparams(("arbitrary",)),
        name="moe_dispatch",
    )(cnt, loc, glb, fill_start, info, h)


def _expert_body(te_ref, nv_ref, xs_ref, wg_ref, wu_ref, wd_ref, ys_ref, acc_ref):
    t = pl.program_id(0)
    j = pl.program_id(1)

    @pl.when(t < nv_ref[0])
    def _():
        x = xs_ref[...].astype(BF16)
        gate = jnp.dot(x, wg_ref[0], preferred_element_type=F32)
        up = jnp.dot(x, wu_ref[0], preferred_element_type=F32)
        act = (gate * _sigmoid(gate) * up).astype(BF16)
        y = jnp.dot(act, wd_ref[0], preferred_element_type=F32)

        @pl.when(j == 0)
        def _():
            acc_ref[...] = y

        @pl.when(j > 0)
        def _():
            acc_ref[...] += y

        @pl.when(j == pl.num_programs(1) - 1)
        def _():
            ys_ref[...] = acc_ref[...]

    @pl.when(t >= nv_ref[0])
    def _():
        ys_ref[...] = jnp.zeros_like(ys_ref)


def _experts(tile_expert, n_valid, xs, wg, wu, wd, tile):
    P, D = xs.shape
    ff = wg.shape[2]
    fs = ff // MOE_FF_SPLIT
    n_tiles = P // tile

    def row_map(t, j, te, nv):
        return (jnp.minimum(t, nv[0] - 1), 0)

    grid_spec = pltpu.PrefetchScalarGridSpec(
        num_scalar_prefetch=2,
        grid=(n_tiles, MOE_FF_SPLIT),
        in_specs=[
            pl.BlockSpec((tile, D), row_map),
            pl.BlockSpec((1, D, fs), lambda t, j, te, nv: (te[t], 0, j)),
            pl.BlockSpec((1, D, fs), lambda t, j, te, nv: (te[t], 0, j)),
            pl.BlockSpec((1, fs, D), lambda t, j, te, nv: (te[t], j, 0)),
        ],
        out_specs=pl.BlockSpec((tile, D), lambda t, j, te, nv: (t, 0)),
        scratch_shapes=[pltpu.VMEM((tile, D), F32)],
    )
    return pl.pallas_call(
        _expert_body,
        grid_spec=grid_spec,
        out_shape=jax.ShapeDtypeStruct((P, D), F32),
        compiler_params=_cparams(("arbitrary", "arbitrary")),
        name="moe_experts",
    )(tile_expert, n_valid, xs, wg, wu, wd)


def _combine_body(cnt_ref, loc_ref, glb_ref, x1_ref, info_ref, ys_ref, *rest, final):
    if final:
        fg_ref, o_ref, local_ref, sem = rest
    else:
        o_ref, local_ref, sem = rest
    i = pl.program_id(0)
    n = pl.num_programs(0)
    slot = i % 2

    def copies(step, buf, act):
        for e in range(N_EXPERTS):
            j = step * N_EXPERTS + e
            _group_copies(cnt_ref[j], ys_ref, glb_ref[j], local_ref.at[buf], loc_ref[j], sem.at[buf], act)

    @pl.when(i == 0)
    def _():
        local_ref[...] = jnp.zeros_like(local_ref)
        copies(0, 0, lambda cp: cp.start())

    @pl.when(i + 1 < n)
    def _():
        copies(i + 1, 1 - slot, lambda cp: cp.start())

    copies(i, slot, lambda cp: cp.wait())

    info = info_ref[...]
    rows = local_ref.shape[1]
    lane = lax.broadcasted_iota(jnp.int32, (info.shape[0], rows), 1).astype(F32)
    weights = (jnp.where(lane == info[:, 0:1], info[:, 2:3], 0.0)
               + jnp.where(lane == info[:, 1:2], info[:, 3:4], 0.0))
    w_hi = weights.astype(BF16)
    w_lo = (weights - w_hi.astype(F32)).astype(BF16)
    ys = local_ref[slot].astype(BF16)
    y = (x1_ref[...] + jnp.dot(w_hi, ys, preferred_element_type=F32)
         + jnp.dot(w_lo, ys, preferred_element_type=F32))
    if final:
        y = _rms(y, fg_ref[...])
    o_ref[...] = y


def _combine(cnt, loc, glb, x1, info, ys, final_gain, local_rows):
    T, D = x1.shape
    tm = min(TOK_TILE, T)
    final = final_gain is not None
    tok = lambda i, *_: (i, 0)
    in_specs = [
        pl.BlockSpec((tm, D), tok),
        pl.BlockSpec((tm, LANES), tok),
        pl.BlockSpec(memory_space=pl.ANY),
    ]
    args = [cnt, loc, glb, x1, info, ys]
    if final:
        in_specs.append(pl.BlockSpec((1, D), lambda i, *_: (0, 0)))
        args.append(final_gain)
    grid_spec = pltpu.PrefetchScalarGridSpec(
        num_scalar_prefetch=3,
        grid=(T // tm,),
        in_specs=in_specs,
        out_specs=pl.BlockSpec((tm, D), tok),
        scratch_shapes=[pltpu.VMEM((2, local_rows, D), F32), pltpu.SemaphoreType.DMA((2,))],
    )
    return pl.pallas_call(
        functools.partial(_combine_body, final=final),
        grid_spec=grid_spec,
        out_shape=jax.ShapeDtypeStruct((T, D), F32),
        compiler_params=_cparams(("arbitrary",)),
        name="moe_combine",
    )(*args)


def _moe_layer(x2, a, r, w_out, gain, w_router, wg, wu, wd, final_gain):
    T, D = x2.shape
    tile = min(MOE_TM, T)
    tm = min(TOK_TILE, T)
    nt = T // tm
    wr = jnp.zeros((D, LANES), F32).at[:, :N_EXPERTS].set(w_router)
    x1, h, route = _router(x2, a, r, w_out, gain, wr)

    experts = jnp.arange(N_EXPERTS, dtype=jnp.int32)
    e = route[:, :TOP_K].astype(jnp.int32).reshape(nt, tm, TOP_K)
    oh0 = (e[:, :, 0, None] == experts).astype(jnp.int32)
    oh1 = (e[:, :, 1, None] == experts).astype(jnp.int32)
    before0 = jnp.cumsum(oh0, axis=1) - oh0
    before1 = jnp.cumsum(oh1, axis=1) - oh1
    cnt0 = jnp.sum(oh0, axis=1)
    cnt = ((cnt0 + jnp.sum(oh1, axis=1) + GROUP_ALIGN - 1) // GROUP_ALIGN) * GROUP_ALIGN
    loc = jnp.cumsum(cnt, axis=1) - cnt
    lpos0 = jnp.sum(oh0 * (loc[:, None, :] + before0), axis=-1)
    lpos1 = jnp.sum(oh1 * (loc[:, None, :] + cnt0[:, None, :] + before1), axis=-1)
    total = jnp.sum(cnt, axis=0)
    padded = ((total + tile - 1) // tile) * tile
    ends = jnp.cumsum(padded)
    glb = (ends - padded)[None, :] + jnp.cumsum(cnt, axis=0) - cnt

    local_rows = -(-(tm * TOP_K + N_EXPERTS * (GROUP_ALIGN - 1)) // LANES) * LANES
    n_rows = -(-(T * TOP_K + nt * N_EXPERTS * (GROUP_ALIGN - 1)) // tile) * tile + N_EXPERTS * tile
    tile_row = jnp.arange(n_rows // tile, dtype=jnp.int32) * tile
    tail_row = tile_row[(T * TOP_K) // tile:]
    fill_start = jnp.concatenate([jnp.where(padded > 0, ends - tile, -1),
                                  jnp.where(tail_row >= ends[-1], tail_row, -1)]).astype(jnp.int32)
    tile_expert = jnp.minimum(jnp.sum(tile_row[:, None] >= ends[None, :], axis=1), N_EXPERTS - 1).astype(jnp.int32)
    n_valid = (ends[-1:] // tile).astype(jnp.int32)
    info = jnp.concatenate([lpos0.reshape(T, 1).astype(F32), lpos1.reshape(T, 1).astype(F32),
                            route[:, 2:4], jnp.zeros((T, LANES - 4), F32)], axis=1)
    cnt, loc, glb = (t.reshape(-1).astype(jnp.int32) for t in (cnt, loc, glb))

    xs = _dispatch(cnt, loc, glb, fill_start, info, h, n_rows, tile, local_rows)
    ys = _experts(tile_expert, n_valid, xs, wg, wu, wd, tile)
    return _combine(cnt, loc, glb, x1, info, ys, final_gain, local_rows)


def _rotate_half_cols(w):
    half = QK_ROPE // 2
    return jnp.concatenate([-w[..., half:], w[..., :half]], axis=-1)


def _prep_w_in(w):
    o = Q_RANK + KV_RANK
    k_pe = w[:, o:o + QK_ROPE]
    return jnp.concatenate([w[:, :o], w[:, o + QK_ROPE:], k_pe, _rotate_half_cols(k_pe)], axis=1).astype(BF16)


def _prep_w_q(w):
    w3 = w.reshape(Q_RANK, MLA_HEADS, QK_DIM)
    nope = w3[:, :, :QK_NOPE].reshape(Q_RANK, MLA_HEADS * QK_NOPE)
    pe = w3[:, :, QK_NOPE:]
    pair = jnp.concatenate([pe, _rotate_half_cols(pe)], axis=-1).reshape(Q_RANK, MLA_HEADS * 2 * QK_ROPE)
    return jnp.concatenate([nope, pair], axis=1).astype(BF16)


def _prep_w_kv(w):
    w4 = w.reshape(KV_RANK, MLA_HEADS, 2, QK_NOPE)
    return w4.transpose(0, 2, 1, 3).reshape(KV_RANK, 2 * MLA_HEADS * QK_NOPE).astype(BF16)


def kernel(x, positions, mix_norm, w_in, q_a_norm, w_q_b, kv_a_norm, w_kv_b, hg_lower_bounds, hg_out_norm, w_out, ffn_norm, dense_w_gate, dense_w_up, dense_w_down, moe_router, moe_w_gate, moe_w_up, moe_w_down, final_norm):
    B, S, D = x.shape
    depth = w_in.shape[0]
    lb_p = jax.nn.softmax(hg_lower_bounds.astype(F32), axis=0)
    lower = jnp.cumsum(lb_p, axis=0) - lb_p[0:1]
    inv = ROPE_THETA ** (-jnp.arange(0, QK_ROPE, 2, dtype=F32) / QK_ROPE)
    invf = jnp.tile(inv, 2 * QK_ROPE // inv.shape[0]).reshape(1, LANES)
    pos3 = positions.reshape(B, S, 1)
    final_gain = final_norm.reshape(1, D)

    x2 = x.reshape(B * S, D)
    for l in range(depth):
        q, k, v, hqig, hf = _mix_in(
            x2.reshape(B, S, D), pos3, mix_norm[l].reshape(1, D), _prep_w_in(w_in[l]),
            q_a_norm[l].reshape(1, Q_RANK), _prep_w_q(w_q_b[l]),
            kv_a_norm[l].reshape(1, KV_RANK), _prep_w_kv(w_kv_b[l]), invf)
        a = _attention(q, k, v).reshape(B * S, MLA_WIDTH)
        r = _hgrn(hqig, hf, lower[l].reshape(1, HG_FDIM), hg_out_norm[l].reshape(1, HG_VAL), B, S)
        fg = final_gain if l == depth - 1 else None
        wo = w_out[l].astype(BF16)
        gain = ffn_norm[l].reshape(1, D)
        j = l // 2
        if l % 2 == 0:
            x2 = _dense_layer(x2, a, r, wo, gain, dense_w_gate[j].astype(BF16), dense_w_up[j].astype(BF16),
                              dense_w_down[j].astype(BF16), fg)
        else:
            x2 = _moe_layer(x2, a, r, wo, gain, moe_router[j], moe_w_gate[j].astype(BF16),
                            moe_w_up[j].astype(BF16), moe_w_down[j].astype(BF16), fg)
    return x2.reshape(B, S, D)
```

```python
import functools

import numpy as np
import jax
import jax.numpy as jnp
from jax import lax
from jax.experimental import pallas as pl
from jax.experimental.pallas import tpu as pltpu

F32 = jnp.float32
BF16 = jnp.bfloat16

D_MODEL = 1024
MLA_HEADS = 4
QK_NOPE = 128
QK_ROPE = 64
QK_DIM = QK_NOPE + QK_ROPE
V_HEAD = 128
Q_RANK = 384
KV_RANK = 256
ROPE_THETA = 10000.0
HG_HEADS = 4
HG_KEY = 128
HG_VAL = 128
HG_CHUNK = 64
HG_SUB = 8
HG_FDIM = HG_HEADS * HG_KEY
HG_WIDTH = HG_HEADS * HG_VAL
MLA_WIDTH = MLA_HEADS * V_HEAD
N_EXPERTS = 8
TOP_K = 2
EPS = 1e-6
F_FLOOR = 1e-30
NEG_BIG = float(np.finfo(np.float32).min)
LOG2_E = float(np.log2(np.e))

LANES = 128
VMEM_LIMIT = 56 * 1024 * 1024

TOK_TILE = 512
ATTN_HEADS_PER_STEP = 2
ROW_SLABS = 2
HG_BLOCK = 512
MOE_TM = 512
MOE_FF_SPLIT = 2
SUBLANES = 8
GROUP_ALIGN = SUBLANES
GROUP_CHUNKS = tuple(GROUP_ALIGN << p for p in reversed(range((TOK_TILE // GROUP_ALIGN).bit_length())))


def _cparams(sem):
    return pltpu.CompilerParams(dimension_semantics=sem, vmem_limit_bytes=VMEM_LIMIT)


def _rms(x, g):
    ms = jnp.mean(x * x, axis=-1, keepdims=True)
    return x * lax.rsqrt(ms + EPS) * g


def _sigmoid(x):
    return 1.0 / (1.0 + jnp.exp(-x))


def _resident(shape):
    nd = len(shape)
    return pl.BlockSpec(shape, lambda *_: (0,) * nd, pipeline_mode=pl.Buffered(1))


def _rope_body(pos_ref, invf_ref, cs_ref):
    ang = pos_ref[...].astype(F32) * invf_ref[...]
    lane = lax.broadcasted_iota(jnp.int32, ang.shape, 1)
    cs_ref[...] = jnp.where(lane < QK_ROPE, jnp.cos(ang), jnp.sin(ang))


def _rope_table(pos2, invf):
    T = pos2.shape[0]
    tm = min(TOK_TILE, T)
    return pl.pallas_call(
        _rope_body,
        grid=(T // tm,),
        in_specs=[pl.BlockSpec((tm, 1), lambda i: (i, 0)), _resident((1, LANES))],
        out_specs=pl.BlockSpec((tm, LANES), lambda i: (i, 0)),
        out_shape=jax.ShapeDtypeStruct((T, LANES), F32),
        compiler_params=_cparams(("parallel",)),
        name="rope_table",
    )(pos2, invf)


def _mix_in_body(x_ref, cs_ref, g_ref, win_ref, qg_ref, wq_ref, kvg_ref, wkv_ref,
                 q_ref, k_ref, v_ref, hqig_ref, hf_ref):
    scale = QK_DIM ** -0.5 * LOG2_E
    tn = (((0,), (1,)), ((), ()))
    nw = MLA_HEADS * QK_NOPE
    slab = x_ref.shape[1] // ROW_SLABS
    for s in range(ROW_SLABS):
        rows = slice(s * slab, (s + 1) * slab)
        h = _rms(x_ref[0, rows, :], g_ref[...]).astype(BF16)

        def proj(a, b):
            return jnp.dot(h, win_ref[:, a:b], preferred_element_type=F32)

        cs = cs_ref[rows, :]

        def rope(pair):
            p = pair * cs
            return p[:, :QK_ROPE] + p[:, QK_ROPE:]

        o = 0
        c_q = proj(o, o + Q_RANK); o += Q_RANK
        c_kv = proj(o, o + KV_RANK); o += KV_RANK
        hqig_ref[rows, 0:HG_FDIM] = proj(o, o + HG_FDIM).astype(BF16); o += HG_FDIM
        hf_ref[rows, :] = proj(o, o + HG_FDIM); o += HG_FDIM
        hqig_ref[rows, HG_FDIM:HG_FDIM + HG_WIDTH] = proj(o, o + HG_WIDTH).astype(BF16); o += HG_WIDTH
        hqig_ref[rows, HG_FDIM + HG_WIDTH:] = proj(o, o + HG_WIDTH).astype(BF16); o += HG_WIDTH
        k_pe = rope(proj(o, o + 2 * QK_ROPE)).astype(BF16)

        qn = _rms(c_q, qg_ref[...]).astype(BF16)
        kvn = _rms(c_kv, kvg_ref[...]).astype(BF16)
        qf_t = lax.dot_general(wq_ref[...], qn, tn, preferred_element_type=F32)
        kf = jnp.dot(kvn, wkv_ref[:, 0:nw], preferred_element_type=F32)
        vf_t = lax.dot_general(wkv_ref[:, nw:], kvn, tn, preferred_element_type=F32)
        cs_t = cs.T
        for hd in range(MLA_HEADS):
            q_ref[0, hd, 0:QK_NOPE, rows] = (qf_t[hd * QK_NOPE:(hd + 1) * QK_NOPE, :] * scale).astype(BF16)
            pr = qf_t[nw + hd * 2 * QK_ROPE: nw + (hd + 1) * 2 * QK_ROPE, :] * cs_t
            q_ref[0, hd, QK_NOPE:QK_DIM, rows] = ((pr[:QK_ROPE, :] + pr[QK_ROPE:, :]) * scale).astype(BF16)
            k_ref[0, hd, rows, 0:QK_NOPE] = kf[:, hd * QK_NOPE:(hd + 1) * QK_NOPE].astype(BF16)
            k_ref[0, hd, rows, QK_NOPE:QK_DIM] = k_pe
            v_ref[0, hd, 0, :, rows] = vf_t[hd * V_HEAD:(hd + 1) * V_HEAD, :].astype(BF16)


def _mix_in(x, cs, gain, w_in, q_gain, w_q, kv_gain, w_kv):
    B, S, D = x.shape
    tm = min(TOK_TILE, S)
    ns = S // tm
    T = B * S
    ncol = w_in.shape[1]
    tok = lambda b, i: (b * ns + i, 0)
    return pl.pallas_call(
        _mix_in_body,
        grid=(B, ns),
        in_specs=[
            pl.BlockSpec((1, tm, D), lambda b, i: (b, i, 0)),
            pl.BlockSpec((tm, LANES), tok),
            _resident((1, D)),
            _resident((D, ncol)),
            _resident((1, Q_RANK)),
            _resident(w_q.shape),
            _resident((1, KV_RANK)),
            _resident(w_kv.shape),
        ],
        out_specs=[
            pl.BlockSpec((1, MLA_HEADS, QK_DIM, tm), lambda b, i: (b, 0, 0, i)),
            pl.BlockSpec((1, MLA_HEADS, tm, QK_DIM), lambda b, i: (b, 0, i, 0)),
            pl.BlockSpec((1, MLA_HEADS, 1, V_HEAD, tm), lambda b, i: (b, 0, i, 0, 0)),
            pl.BlockSpec((tm, HG_FDIM + 2 * HG_WIDTH), tok),
            pl.BlockSpec((tm, HG_FDIM), tok),
        ],
        out_shape=[
            jax.ShapeDtypeStruct((B, MLA_HEADS, QK_DIM, S), BF16),
            jax.ShapeDtypeStruct((B, MLA_HEADS, S, QK_DIM), BF16),
            jax.ShapeDtypeStruct((B, MLA_HEADS, ns, V_HEAD, tm), BF16),
            jax.ShapeDtypeStruct((T, HG_FDIM + 2 * HG_WIDTH), BF16),
            jax.ShapeDtypeStruct((T, HG_FDIM), F32),
        ],
        compiler_params=_cparams(("parallel", "parallel")),
        name="mix_in",
    )(x, cs, gain, w_in, q_gain, w_q, kv_gain, w_kv)


def _attn_body(q_ref, k_ref, v_ref, o_ref, s0_ref, s1_ref, m_ref, l_ref, acc_ref, *, tq, tk, nh):
    qi = pl.program_id(2)

    heads = range(nh)

    def scores(kb, s_ref):
        start = pl.multiple_of(kb * tk, tk)
        for hd in heads:
            s_ref[hd] = jnp.dot(k_ref[0, hd, pl.ds(start, tk), :], q_ref[0, hd],
                                preferred_element_type=F32)

    def update(kb, s_of_head):
        for hd in heads:
            s = s_of_head(hd)
            m = m_ref[hd]
            m_new = jnp.maximum(m, jnp.max(s, axis=0, keepdims=True))
            alpha = jnp.exp2(m - m_new)
            p = jnp.exp2(s - m_new)
            m_ref[hd] = m_new
            l_ref[hd] = alpha * l_ref[hd] + jnp.sum(p, axis=0, keepdims=True)
            acc_ref[hd] = alpha * acc_ref[hd] + jnp.dot(v_ref[0, hd, kb], p.astype(BF16),
                                                        preferred_element_type=F32)

    def masked(s_ref, kb):
        def get(hd):
            s = s_ref[hd]
            key = lax.broadcasted_iota(jnp.int32, s.shape, 0) + (kb * tk - qi * tq)
            qry = lax.broadcasted_iota(jnp.int32, s.shape, 1)
            return jnp.where(key <= qry, s, NEG_BIG)
        return get

    m_ref[...] = jnp.full(m_ref.shape, NEG_BIG, F32)
    l_ref[...] = jnp.zeros(l_ref.shape, F32)
    acc_ref[...] = jnp.zeros(acc_ref.shape, F32)
    nfull = qi * (tq // tk)
    scores(0, s0_ref)

    def pair(j, carry):
        scores(2 * j + 1, s1_ref)
        update(2 * j, lambda hd: s0_ref[hd])
        scores(2 * j + 2, s0_ref)
        update(2 * j + 1, lambda hd: s1_ref[hd])
        return carry

    lax.fori_loop(0, nfull // 2, pair, 0)
    scores(nfull + 1, s1_ref)
    update(nfull, masked(s0_ref, nfull))
    update(nfull + 1, masked(s1_ref, nfull + 1))

    for hd in heads:
        o_ref[0, :, hd * V_HEAD:(hd + 1) * V_HEAD] = (acc_ref[hd] / l_ref[hd]).T.astype(o_ref.dtype)


def _attention(q, k, v):
    B, H, S, _ = k.shape
    tk = v.shape[-1]
    tq = 2 * tk
    nh = ATTN_HEADS_PER_STEP
    return pl.pallas_call(
        functools.partial(_attn_body, tq=tq, tk=tk, nh=nh),
        grid=(B, H // nh, S // tq),
        in_specs=[
            pl.BlockSpec((1, nh, QK_DIM, tq), lambda b, h, i: (b, h, 0, i)),
            pl.BlockSpec((1, nh, S, QK_DIM), lambda b, h, i: (b, h, 0, 0)),
            pl.BlockSpec((1, nh, S // tk, V_HEAD, tk), lambda b, h, i: (b, h, 0, 0, 0)),
        ],
        out_specs=pl.BlockSpec((1, tq, nh * V_HEAD), lambda b, h, i: (b, i, h)),
        out_shape=jax.ShapeDtypeStruct((B, S, H * V_HEAD), BF16),
        scratch_shapes=[
            pltpu.VMEM((nh, tk, tq), F32),
            pltpu.VMEM((nh, tk, tq), F32),
            pltpu.VMEM((nh, 1, tq), F32),
            pltpu.VMEM((nh, 1, tq), F32),
            pltpu.VMEM((nh, V_HEAD, tq), F32),
        ],
        compiler_params=_cparams(("parallel", "parallel", "arbitrary")),
        name="mla_attention",
    )(q, k, v)


def _hgrn_body(hqig_ref, hf_ref, lb_ref, gain_ref, r_ref, state_ref, *, n_chunks):
    C = HG_CHUNK

    @pl.when(pl.program_id(1) == 0)
    def _():
        state_ref[...] = jnp.zeros_like(state_ref)

    row = lax.broadcasted_iota(jnp.int32, (C, C), 0)
    col = lax.broadcasted_iota(jnp.int32, (C, C), 1)
    tri = jnp.where(col <= row, 1.0, 0.0).astype(BF16)
    lb = lb_ref[...]
    gain = gain_ref[...]

    levels = []
    L = C // 2
    while L >= HG_SUB:
        levels.append(L)
        L //= 2
    level_masks = [((row // L) % 2 == 1) & ((col // L) % 2 == 0) & (row // (2 * L) == col // (2 * L))
                   for L in levels]
    frow = lax.broadcasted_iota(jnp.int32, (C, HG_FDIM), 0)
    later_half = [(frow // L) % 2 == 1 for L in levels]
    diag_mask = (col // HG_SUB == row // HG_SUB) & (col % HG_SUB <= row % HG_SUB)
    src = lax.broadcasted_iota(jnp.int32, (HG_SUB * HG_KEY, C), 0) // HG_KEY
    dst = lax.broadcasted_iota(jnp.int32, (HG_SUB * HG_KEY, C), 1) % HG_SUB
    spread = jnp.where(src == dst, 1.0, 0.0).astype(BF16)

    def chunk(c, carry):
        r0 = pl.multiple_of(c * C, C)
        rows = pl.ds(r0, C)
        q_raw = hqig_ref[rows, 0:HG_FDIM].astype(F32)
        z = hf_ref[rows, :]
        q = q_raw * _sigmoid(q_raw)
        sig = _sigmoid(z)
        f = lb + (1.0 - lb) * sig
        log_f = jnp.log2(jnp.maximum(f, F_FLOOR))
        k = (1.0 - lb) * (1.0 - sig)

        p0 = log_f.astype(BF16)
        e0 = log_f - p0.astype(F32)
        p1 = e0.astype(BF16)
        p2 = (e0 - p1.astype(F32)).astype(BF16)
        b = (jnp.dot(tri, p0, preferred_element_type=F32)
             + jnp.dot(tri, p1, preferred_element_type=F32)
             + jnp.dot(tri, p2, preferred_element_type=F32))
        b_last = b[C - 1:C, :]
        q_dec = (q * jnp.exp2(b)).astype(BF16)
        k_dec = (k * jnp.exp2(b_last - b)).astype(BF16)
        s_dec = jnp.exp2(b_last)

        nsub = C // HG_SUB
        b3 = b.reshape(nsub, HG_SUB, HG_FDIM)
        q3 = q.reshape(nsub, HG_SUB, HG_FDIM)
        k3 = k.reshape(nsub, HG_SUB, HG_FDIM)
        diag = []
        for s in range(HG_SUB):
            e = jnp.exp2(jnp.minimum(b3 - b3[:, s:s + 1, :], 0.0))
            diag.append((q3 * e * k3[:, s:s + 1, :]).reshape(C, HG_FDIM).astype(BF16))

        lvl = []
        for L, later in zip(levels, later_half):
            g = C // (2 * L)
            bg = b.reshape(g, 2 * L, HG_FDIM)
            ref = jnp.broadcast_to(bg[:, L - 1:L, :], bg.shape).reshape(C, HG_FDIM)
            lvl.append((jnp.where(later, q, k) * jnp.exp2(-jnp.abs(b - ref))).astype(BF16))

        for hd in range(HG_HEADS):
            lo = hd * HG_KEY
            hs = slice(lo, lo + HG_KEY)
            dp = jnp.concatenate([p[:, hs] for p in diag], axis=1)
            a = jnp.where(diag_mask, jnp.dot(dp, spread, preferred_element_type=F32), 0.0)
            for i in range(len(levels)):
                mh = lvl[i][:, hs]
                sc = lax.dot_general(mh, mh, (((1,), (1,)), ((), ())), preferred_element_type=F32)
                a = a + jnp.where(level_masks[i], sc, 0.0)
            v = hqig_ref[rows, HG_FDIM + hd * HG_VAL: HG_FDIM + (hd + 1) * HG_VAL]
            st = state_ref[hd]
            o = jnp.dot(a.astype(BF16), v, preferred_element_type=F32)
            o = o + lax.dot_general(q_dec[:, hs], st.astype(BF16), (((1,), (1,)), ((), ())),
                                    preferred_element_type=F32)
            state_ref[hd] = s_dec[:, hs] * st + lax.dot_general(
                v, k_dec[:, hs], (((0,), (0,)), ((), ())), preferred_element_type=F32)
            g_raw = hqig_ref[rows, HG_FDIM + HG_WIDTH + hd * HG_VAL:
                             HG_FDIM + HG_WIDTH + (hd + 1) * HG_VAL].astype(F32)
            y = _rms(o, gain) * (g_raw * _sigmoid(g_raw))
            r_ref[rows, hd * HG_VAL:(hd + 1) * HG_VAL] = y.astype(r_ref.dtype)
        return carry

    lax.fori_loop(0, n_chunks, chunk, 0, unroll=4)


def _hgrn(hqig, hf, lower, out_gain, B, S):
    tb = min(HG_BLOCK, S)
    nb = S // tb
    T = B * S
    tok = lambda b, i: (b * nb + i, 0)
    return pl.pallas_call(
        functools.partial(_hgrn_body, n_chunks=tb // HG_CHUNK),
        grid=(B, nb),
        in_specs=[
            pl.BlockSpec((tb, HG_FDIM + 2 * HG_WIDTH), tok),
            pl.BlockSpec((tb, HG_FDIM), tok),
            _resident((1, HG_FDIM)),
            _resident((1, HG_VAL)),
        ],
        out_specs=pl.BlockSpec((tb, HG_WIDTH), tok),
        out_shape=jax.ShapeDtypeStruct((T, HG_WIDTH), BF16),
        scratch_shapes=[pltpu.VMEM((HG_HEADS, HG_VAL, HG_KEY), F32)],
        compiler_params=_cparams(("parallel", "arbitrary")),
        name="hgrn2_scan",
    )(hqig, hf, lower, out_gain)


def _mix_out(x_ref, a_ref, r_ref, wo_ref, rows=slice(None)):
    y = jnp.dot(a_ref[rows, :], wo_ref[0:MLA_WIDTH, :], preferred_element_type=F32)
    y = y + jnp.dot(r_ref[rows, :], wo_ref[MLA_WIDTH:, :], preferred_element_type=F32)
    return x_ref[rows, :] + y


def _dense_body(x_ref, a_ref, r_ref, wo_ref, g_ref, wg_ref, wu_ref, wd_ref, *rest, n_split, final):
    if final:
        fg_ref, o_ref = rest
    else:
        (o_ref,) = rest
    x1 = _mix_out(x_ref, a_ref, r_ref, wo_ref)
    h = _rms(x1, g_ref[...]).astype(BF16)
    ff = wg_ref.shape[1]
    step = ff // n_split
    y = x1
    for j in range(n_split):
        cs = slice(j * step, (j + 1) * step)
        gate = jnp.dot(h, wg_ref[:, cs], preferred_element_type=F32)
        up = jnp.dot(h, wu_ref[:, cs], preferred_element_type=F32)
        act = (gate * _sigmoid(gate) * up).astype(BF16)
        y = y + jnp.dot(act, wd_ref[cs, :], preferred_element_type=F32)
    if final:
        y = _rms(y, fg_ref[...])
    o_ref[...] = y


def _dense_layer(x2, a, r, w_out, gain, wg, wu, wd, final_gain):
    T, D = x2.shape
    tm = min(TOK_TILE, T)
    ff = wg.shape[1]
    tok = lambda i: (i, 0)
    final = final_gain is not None
    in_specs = [
        pl.BlockSpec((tm, D), tok),
        pl.BlockSpec((tm, MLA_WIDTH), tok),
        pl.BlockSpec((tm, HG_WIDTH), tok),
        _resident(w_out.shape),
        _resident((1, D)),
        _resident((D, ff)),
        _resident((D, ff)),
        _resident((ff, D)),
    ]
    args = [x2, a, r, w_out, gain, wg, wu, wd]
    if final:
        in_specs.append(_resident((1, D)))
        args.append(final_gain)
    return pl.pallas_call(
        functools.partial(_dense_body, n_split=2, final=final),
        grid=(T // tm,),
        in_specs=in_specs,
        out_specs=pl.BlockSpec((tm, D), tok),
        out_shape=jax.ShapeDtypeStruct((T, D), F32),
        compiler_params=_cparams(("parallel",)),
        name="mix_out_dense_ffn",
    )(*args)


def _router_body(x_ref, a_ref, r_ref, wo_ref, g_ref, wr_hi_ref, wr_lo_ref, x1_ref, h_ref, route_ref):
    tm = x_ref.shape[0]
    slab = tm // ROW_SLABS
    for s in range(ROW_SLABS):
        rows = slice(s * slab, (s + 1) * slab)
        x1 = _mix_out(x_ref, a_ref, r_ref, wo_ref, rows)
        x1_ref[rows, :] = x1
        h = _rms(x1, g_ref[...])
        h_hi = h.astype(BF16)
        h_ref[rows, :] = h_hi
        h_lo = (h - h_hi.astype(F32)).astype(BF16)
        logits = (jnp.dot(h_hi, wr_hi_ref[...], preferred_element_type=F32)
                  + jnp.dot(h_hi, wr_lo_ref[...], preferred_element_type=F32)
                  + jnp.dot(h_lo, wr_hi_ref[...], preferred_element_type=F32))
        lg = logits.T[0:N_EXPERTS, :]
        idx = lax.broadcasted_iota(jnp.int32, lg.shape, 0)
        neg = -jnp.inf
        m1 = jnp.max(lg, axis=0, keepdims=True)
        i1 = jnp.min(jnp.where(lg == m1, idx, N_EXPERTS), axis=0, keepdims=True)
        lg2 = jnp.where(idx == i1, neg, lg)
        m2 = jnp.max(lg2, axis=0, keepdims=True)
        i2 = jnp.min(jnp.where(lg2 == m2, idx, N_EXPERTS), axis=0, keepdims=True)
        e2 = jnp.exp(m2 - m1)
        g1 = 1.0 / (1.0 + e2)
        g2 = e2 / (1.0 + e2)
        route_ref[:, rows] = jnp.concatenate(
            [i1.astype(F32), i2.astype(F32), g1, g2, jnp.zeros((SUBLANES - 2 * TOP_K, slab), F32)], axis=0)


def _router(x2, a, r, w_out, gain, w_router):
    T, D = x2.shape
    tm = min(TOK_TILE, T)
    tok = lambda i: (i, 0)
    return pl.pallas_call(
        _router_body,
        grid=(T // tm,),
        in_specs=[
            pl.BlockSpec((tm, D), tok),
            pl.BlockSpec((tm, MLA_WIDTH), tok),
            pl.BlockSpec((tm, HG_WIDTH), tok),
            _resident(w_out.shape),
            _resident((1, D)),
            _resident((D, LANES)),
            _resident((D, LANES)),
        ],
        out_specs=[pl.BlockSpec((tm, D), tok), pl.BlockSpec((tm, D), tok),
                   pl.BlockSpec((SUBLANES, tm), lambda i: (0, i))],
        out_shape=[jax.ShapeDtypeStruct((T, D), F32), jax.ShapeDtypeStruct((T, D), BF16),
                   jax.ShapeDtypeStruct((SUBLANES, T), F32)],
        compiler_params=_cparams(("parallel",)),
        name="mix_out_router",
    )(x2, a, r, w_out, gain, w_router.astype(BF16),
      (w_router - w_router.astype(BF16).astype(F32)).astype(BF16))


def _group_copies(count, src_ref, src_off, dst_ref, dst_off, sem, act):
    done = 0
    for p in GROUP_CHUNKS:
        take = (count & p) != 0

        @pl.when(take)
        def _():
            src = src_ref.at[pl.ds(pl.multiple_of(src_off + done, GROUP_ALIGN), p), :]
            dst = dst_ref.at[pl.ds(pl.multiple_of(dst_off + done, GROUP_ALIGN), p), :]
            act(pltpu.make_async_copy(src, dst, sem))

        done = done + jnp.where(take, p, 0)


def _dispatch_body(cnt_ref, loc_ref, glb_ref, fill_ref, info_ref, h_ref, xs_ref,
                   sorted_ref, zero_ref, sem, fill_sem, *, tile):
    i = pl.program_id(0)
    n = pl.num_programs(0)
    slot = i % 2

    @pl.when(i == 0)
    def _():
        zero_ref[...] = jnp.zeros_like(zero_ref)
        for e in range(fill_ref.shape[0]):
            @pl.when(fill_ref[e] >= 0)
            def _():
                cp = pltpu.make_async_copy(
                    zero_ref, xs_ref.at[pl.ds(pl.multiple_of(fill_ref[e], tile), tile), :], fill_sem)
                cp.start()
                cp.wait()

    info = info_ref[...]
    rows = sorted_ref.shape[1]
    lane = lax.broadcasted_iota(jnp.int32, (info.shape[0], rows), 1).astype(F32)
    onehot = jnp.where((lane == info[:, 0:1]) | (lane == info[:, 1:2]), 1.0, 0.0).astype(BF16)
    sorted_ref[slot] = lax.dot_general(onehot, h_ref[...], (((0,), (0,)), ((), ())),
                                       preferred_element_type=F32)

    def copies(step, buf, act):
        for e in range(N_EXPERTS):
            j = step * N_EXPERTS + e
            _group_copies(cnt_ref[j], sorted_ref.at[buf], loc_ref[j], xs_ref, glb_ref[j], sem.at[buf], act)

    copies(i, slot, lambda cp: cp.start())

    @pl.when(i > 0)
    def _():
        copies(i - 1, 1 - slot, lambda cp: cp.wait())

    @pl.when(i == n - 1)
    def _():
        copies(i, slot, lambda cp: cp.wait())


def _dispatch(cnt, loc, glb, fill_start, info, h, n_rows, tile, local_rows):
    T, D = h.shape
    tm = min(TOK_TILE, T)
    tok = lambda i, *_: (i, 0)
    grid_spec = pltpu.PrefetchScalarGridSpec(
        num_scalar_prefetch=4,
        grid=(T // tm,),
        in_specs=[pl.BlockSpec((tm, LANES), tok), pl.BlockSpec((tm, D), tok)],
        out_specs=pl.BlockSpec(memory_space=pl.ANY),
        scratch_shapes=[pltpu.VMEM((2, local_rows, D), F32), pltpu.VMEM((tile, D), F32),
                        pltpu.SemaphoreType.DMA((2,)), pltpu.SemaphoreType.DMA(())],
    )
    return pl.pallas_call(
        functools.partial(_dispatch_body, tile=tile),
        grid_spec=grid_spec,
        out_shape=jax.ShapeDtypeStruct((n_rows, D), F32),
        compiler_params=_cparams(("arbitrary",)),
        name="moe_dispatch",
    )(cnt, loc, glb, fill_start, info, h)


def _expert_body(te_ref, nv_ref, xs_ref, wg_ref, wu_ref, wd_ref, ys_ref, acc_ref):
    t = pl.program_id(0)
    j = pl.program_id(1)

    @pl.when(t < nv_ref[0])
    def _():
        x = xs_ref[...].astype(BF16)
        gate = jnp.dot(x, wg_ref[0], preferred_element_type=F32)
        up = jnp.dot(x, wu_ref[0], preferred_element_type=F32)
        act = (gate * _sigmoid(gate) * up).astype(BF16)
        y = jnp.dot(act, wd_ref[0], preferred_element_type=F32)

        @pl.when(j == 0)
        def _():
            acc_ref[...] = y

        @pl.when(j > 0)
        def _():
            acc_ref[...] += y

        @pl.when(j == pl.num_programs(1) - 1)
        def _():
            ys_ref[...] = acc_ref[...]

    @pl.when(t >= nv_ref[0])
    def _():
        ys_ref[...] = jnp.zeros_like(ys_ref)


def _experts(tile_expert, n_valid, xs, wg, wu, wd, tile):
    P, D = xs.shape
    ff = wg.shape[2]
    fs = ff // MOE_FF_SPLIT
    n_tiles = P // tile

    def row_map(t, j, te, nv):
        return (jnp.minimum(t, nv[0] - 1), 0)

    grid_spec = pltpu.PrefetchScalarGridSpec(
        num_scalar_prefetch=2,
        grid=(n_tiles, MOE_FF_SPLIT),
        in_specs=[
            pl.BlockSpec((tile, D), row_map),
            pl.BlockSpec((1, D, fs), lambda t, j, te, nv: (te[t], 0, j)),
            pl.BlockSpec((1, D, fs), lambda t, j, te, nv: (te[t], 0, j)),
            pl.BlockSpec((1, fs, D), lambda t, j, te, nv: (te[t], j, 0)),
        ],
        out_specs=pl.BlockSpec((tile, D), lambda t, j, te, nv: (t, 0)),
        scratch_shapes=[pltpu.VMEM((tile, D), F32)],
    )
    return pl.pallas_call(
        _expert_body,
        grid_spec=grid_spec,
        out_shape=jax.ShapeDtypeStruct((P, D), F32),
        compiler_params=_cparams(("arbitrary", "arbitrary")),
        name="moe_experts",
    )(tile_expert, n_valid, xs, wg, wu, wd)


def _combine_body(cnt_ref, loc_ref, glb_ref, x1_ref, info_ref, ys_ref, *rest, final):
    if final:
        fg_ref, o_ref, local_ref, sem = rest
    else:
        o_ref, local_ref, sem = rest
    i = pl.program_id(0)
    n = pl.num_programs(0)
    slot = i % 2

    def copies(step, buf, act):
        for e in range(N_EXPERTS):
            j = step * N_EXPERTS + e
            _group_copies(cnt_ref[j], ys_ref, glb_ref[j], local_ref.at[buf], loc_ref[j], sem.at[buf], act)

    @pl.when(i == 0)
    def _():
        local_ref[...] = jnp.zeros_like(local_ref)
        copies(0, 0, lambda cp: cp.start())

    @pl.when(i + 1 < n)
    def _():
        copies(i + 1, 1 - slot, lambda cp: cp.start())

    copies(i, slot, lambda cp: cp.wait())

    info = info_ref[...]
    rows = local_ref.shape[1]
    lane = lax.broadcasted_iota(jnp.int32, (info.shape[0], rows), 1).astype(F32)
    weights = (jnp.where(lane == info[:, 0:1], info[:, 2:3], 0.0)
               + jnp.where(lane == info[:, 1:2], info[:, 3:4], 0.0))
    w_hi = weights.astype(BF16)
    w_lo = (weights - w_hi.astype(F32)).astype(BF16)
    ys = local_ref[slot].astype(BF16)
    y = (x1_ref[...] + jnp.dot(w_hi, ys, preferred_element_type=F32)
         + jnp.dot(w_lo, ys, preferred_element_type=F32))
    if final:
        y = _rms(y, fg_ref[...])
    o_ref[...] = y


def _combine(cnt, loc, glb, x1, info, ys, final_gain, local_rows):
    T, D = x1.shape
    tm = min(TOK_TILE, T)
    final = final_gain is not None
    tok = lambda i, *_: (i, 0)
    in_specs = [
        pl.BlockSpec((tm, D), tok),
        pl.BlockSpec((tm, LANES), tok),
        pl.BlockSpec(memory_space=pl.ANY),
    ]
    args = [cnt, loc, glb, x1, info, ys]
    if final:
        in_specs.append(pl.BlockSpec((1, D), lambda i, *_: (0, 0)))
        args.append(final_gain)
    grid_spec = pltpu.PrefetchScalarGridSpec(
        num_scalar_prefetch=3,
        grid=(T // tm,),
        in_specs=in_specs,
        out_specs=pl.BlockSpec((tm, D), tok),
        scratch_shapes=[pltpu.VMEM((2, local_rows, D), F32), pltpu.SemaphoreType.DMA((2,))],
    )
    return pl.pallas_call(
        functools.partial(_combine_body, final=final),
        grid_spec=grid_spec,
        out_shape=jax.ShapeDtypeStruct((T, D), F32),
        compiler_params=_cparams(("arbitrary",)),
        name="moe_combine",
    )(*args)


def _moe_layer(x2, a, r, w_out, gain, w_router, wg, wu, wd, final_gain):
    T, D = x2.shape
    tile = min(MOE_TM, T)
    tm = min(TOK_TILE, T)
    nt = T // tm
    wr = jnp.zeros((D, LANES), F32).at[:, :N_EXPERTS].set(w_router)
    x1, h, route = _router(x2, a, r, w_out, gain, wr)

    experts = jnp.arange(N_EXPERTS, dtype=jnp.int32)
    e = route[:TOP_K].astype(jnp.int32).reshape(TOP_K, nt, tm)
    oh0 = (e[0, :, :, None] == experts).astype(jnp.int32)
    oh1 = (e[1, :, :, None] == experts).astype(jnp.int32)
    earlier = jnp.tril(jnp.ones((tm, tm), F32), -1)
    before0 = jnp.einsum('ts,nse->nte', earlier, oh0.astype(F32)).astype(jnp.int32)
    before1 = jnp.einsum('ts,nse->nte', earlier, oh1.astype(F32)).astype(jnp.int32)
    cnt0 = jnp.sum(oh0, axis=1)
    cnt = ((cnt0 + jnp.sum(oh1, axis=1) + GROUP_ALIGN - 1) // GROUP_ALIGN) * GROUP_ALIGN
    loc = jnp.cumsum(cnt, axis=1) - cnt
    lpos0 = jnp.sum(oh0 * (loc[:, None, :] + before0), axis=-1)
    lpos1 = jnp.sum(oh1 * (loc[:, None, :] + cnt0[:, None, :] + before1), axis=-1)
    total = jnp.sum(cnt, axis=0)
    padded = ((total + tile - 1) // tile) * tile
    ends = jnp.cumsum(padded)
    glb = (ends - padded)[None, :] + jnp.cumsum(cnt, axis=0) - cnt

    local_rows = -(-(tm * TOP_K + N_EXPERTS * (GROUP_ALIGN - 1)) // LANES) * LANES
    n_rows = -(-(T * TOP_K + nt * N_EXPERTS * (GROUP_ALIGN - 1)) // tile) * tile + N_EXPERTS * tile
    tile_row = jnp.arange(n_rows // tile, dtype=jnp.int32) * tile
    tail_row = tile_row[(T * TOP_K) // tile:]
    fill_start = jnp.concatenate([jnp.where(padded > 0, ends - tile, -1),
                                  jnp.where(tail_row >= ends[-1], tail_row, -1)]).astype(jnp.int32)
    tile_expert = jnp.minimum(jnp.sum(tile_row[:, None] >= ends[None, :], axis=1), N_EXPERTS - 1).astype(jnp.int32)
    n_valid = (ends[-1:] // tile).astype(jnp.int32)
    info = jnp.concatenate([lpos0.reshape(T, 1).astype(F32), lpos1.reshape(T, 1).astype(F32),
                            route[TOP_K:2 * TOP_K].T, jnp.zeros((T, LANES - 4), F32)], axis=1)
    cnt, loc, glb = (t.reshape(-1).astype(jnp.int32) for t in (cnt, loc, glb))

    xs = _dispatch(cnt, loc, glb, fill_start, info, h, n_rows, tile, local_rows)
    ys = _experts(tile_expert, n_valid, xs, wg, wu, wd, tile)
    return _combine(cnt, loc, glb, x1, info, ys, final_gain, local_rows)


def _rotate_half_cols(w):
    half = QK_ROPE // 2
    return jnp.concatenate([-w[..., half:], w[..., :half]], axis=-1)


def _prep_w_in(w):
    o = Q_RANK + KV_RANK
    k_pe = w[:, o:o + QK_ROPE]
    return jnp.concatenate([w[:, :o], w[:, o + QK_ROPE:], k_pe, _rotate_half_cols(k_pe)], axis=1).astype(BF16)


def _prep_w_q(w):
    w3 = w.reshape(Q_RANK, MLA_HEADS, QK_DIM)
    nope = w3[:, :, :QK_NOPE].reshape(Q_RANK, MLA_HEADS * QK_NOPE)
    pe = w3[:, :, QK_NOPE:]
    pair = jnp.concatenate([pe, _rotate_half_cols(pe)], axis=-1).reshape(Q_RANK, MLA_HEADS * 2 * QK_ROPE)
    return jnp.concatenate([nope, pair], axis=1).astype(BF16)


def _prep_w_kv(w):
    w4 = w.reshape(KV_RANK, MLA_HEADS, 2, QK_NOPE)
    return w4.transpose(0, 2, 1, 3).reshape(KV_RANK, 2 * MLA_HEADS * QK_NOPE).astype(BF16)


def kernel(x, positions, mix_norm, w_in, q_a_norm, w_q_b, kv_a_norm, w_kv_b, hg_lower_bounds, hg_out_norm, w_out, ffn_norm, dense_w_gate, dense_w_up, dense_w_down, moe_router, moe_w_gate, moe_w_up, moe_w_down, final_norm):
    B, S, D = x.shape
    depth = w_in.shape[0]
    lb_p = jax.nn.softmax(hg_lower_bounds.astype(F32), axis=0)
    lower = jnp.cumsum(lb_p, axis=0) - lb_p[0:1]
    inv = ROPE_THETA ** (-jnp.arange(0, QK_ROPE, 2, dtype=F32) / QK_ROPE)
    invf = jnp.tile(inv, 2 * QK_ROPE // inv.shape[0]).reshape(1, LANES)
    cs = _rope_table(positions.reshape(B * S, 1), invf)
    final_gain = final_norm.reshape(1, D)

    x2 = x.reshape(B * S, D)
    for l in range(depth):
        q, k, v, hqig, hf = _mix_in(
            x2.reshape(B, S, D), cs, mix_norm[l].reshape(1, D), _prep_w_in(w_in[l]),
            q_a_norm[l].reshape(1, Q_RANK), _prep_w_q(w_q_b[l]),
            kv_a_norm[l].reshape(1, KV_RANK), _prep_w_kv(w_kv_b[l]))
        a = _attention(q, k, v).reshape(B * S, MLA_WIDTH)
        r = _hgrn(hqig, hf, lower[l].reshape(1, HG_FDIM), hg_out_norm[l].reshape(1, HG_VAL), B, S)
        fg = final_gain if l == depth - 1 else None
        wo = w_out[l].astype(BF16)
        gain = ffn_norm[l].reshape(1, D)
        j = l // 2
        if l % 2 == 0:
            x2 = _dense_layer(x2, a, r, wo, gain, dense_w_gate[j].astype(BF16), dense_w_up[j].astype(BF16),
                              dense_w_down[j].astype(BF16), fg)
        else:
            x2 = _moe_layer(x2, a, r, wo, gain, moe_router[j], moe_w_gate[j].astype(BF16),
                            moe_w_up[j].astype(BF16), moe_w_down[j].astype(BF16), fg)
    return x2.reshape(B, S, D)
```

```python
import functools

import numpy as np
import jax
import jax.numpy as jnp
from jax import lax
from jax.experimental import pallas as pl
from jax.experimental.pallas import tpu as pltpu

F32 = jnp.float32
BF16 = jnp.bfloat16

D_MODEL = 1024
MLA_HEADS = 4
QK_NOPE = 128
QK_ROPE = 64
QK_DIM = QK_NOPE + QK_ROPE
V_HEAD = 128
V_ONES = 16
Q_RANK = 384
KV_RANK = 256
ROPE_THETA = 10000.0
HG_HEADS = 4
HG_KEY = 128
HG_VAL = 128
HG_CHUNK = 64
HG_SUB = 8
HG_FDIM = HG_HEADS * HG_KEY
HG_WIDTH = HG_HEADS * HG_VAL
MLA_WIDTH = MLA_HEADS * V_HEAD
N_EXPERTS = 8
TOP_K = 2
EPS = 1e-6
F_FLOOR = 1e-30
NEG_BIG = float(np.finfo(np.float32).min)
LOG2_E = float(np.log2(np.e))

LANES = 128
VMEM_LIMIT = 56 * 1024 * 1024

TOK_TILE = 512
ATTN_HEADS_PER_STEP = 2
ROW_SLABS = 2
HG_BLOCK = 512
MOE_TM = 512
MOE_FF_SPLIT = 2
SUBLANES = 8
GROUP_ALIGN = SUBLANES
GROUP_CHUNKS = tuple(GROUP_ALIGN << p for p in reversed(range((TOK_TILE // GROUP_ALIGN).bit_length())))


def _cparams(sem):
    return pltpu.CompilerParams(dimension_semantics=sem, vmem_limit_bytes=VMEM_LIMIT)


def _rms(x, g):
    ms = jnp.mean(x * x, axis=-1, keepdims=True)
    return x * lax.rsqrt(ms + EPS) * g


def _sigmoid(x):
    return 1.0 / (1.0 + jnp.exp(-x))


def _resident(shape):
    nd = len(shape)
    return pl.BlockSpec(shape, lambda *_: (0,) * nd, pipeline_mode=pl.Buffered(1))


def _rope_body(pos_ref, invf_ref, cs_ref):
    ang = pos_ref[...].astype(F32) * invf_ref[...]
    lane = lax.broadcasted_iota(jnp.int32, ang.shape, 1)
    cs_ref[...] = jnp.where(lane < QK_ROPE, jnp.cos(ang), jnp.sin(ang))


def _rope_table(pos2, invf):
    T = pos2.shape[0]
    tm = min(TOK_TILE, T)
    return pl.pallas_call(
        _rope_body,
        grid=(T // tm,),
        in_specs=[pl.BlockSpec((tm, 1), lambda i: (i, 0)), _resident((1, LANES))],
        out_specs=pl.BlockSpec((tm, LANES), lambda i: (i, 0)),
        out_shape=jax.ShapeDtypeStruct((T, LANES), F32),
        compiler_params=_cparams(("parallel",)),
        name="rope_table",
    )(pos2, invf)


def _mix_in_body(x_ref, cs_ref, g_ref, win_ref, qg_ref, wq_ref, kvg_ref, wkv_ref,
                 q_ref, k_ref, v_ref, hqig_ref, hf_ref):
    scale = QK_DIM ** -0.5 * LOG2_E
    tn = (((0,), (1,)), ((), ()))
    nw = MLA_HEADS * QK_NOPE
    slab = x_ref.shape[1] // ROW_SLABS
    for s in range(ROW_SLABS):
        rows = slice(s * slab, (s + 1) * slab)
        h = _rms(x_ref[0, rows, :], g_ref[...]).astype(BF16)

        def proj(a, b):
            return jnp.dot(h, win_ref[:, a:b], preferred_element_type=F32)

        cs = cs_ref[rows, :]

        def rope(pair):
            p = pair * cs
            return p[:, :QK_ROPE] + p[:, QK_ROPE:]

        o = 0
        c_q = proj(o, o + Q_RANK); o += Q_RANK
        c_kv = proj(o, o + KV_RANK); o += KV_RANK
        hqig_ref[rows, 0:HG_FDIM] = proj(o, o + HG_FDIM).astype(BF16); o += HG_FDIM
        hf_ref[rows, :] = proj(o, o + HG_FDIM); o += HG_FDIM
        hqig_ref[rows, HG_FDIM:HG_FDIM + HG_WIDTH] = proj(o, o + HG_WIDTH).astype(BF16); o += HG_WIDTH
        hqig_ref[rows, HG_FDIM + HG_WIDTH:] = proj(o, o + HG_WIDTH).astype(BF16); o += HG_WIDTH
        k_pe = rope(proj(o, o + 2 * QK_ROPE)).astype(BF16)

        qn = _rms(c_q, qg_ref[...]).astype(BF16)
        kvn = _rms(c_kv, kvg_ref[...]).astype(BF16)
        qf_t = lax.dot_general(wq_ref[...], qn, tn, preferred_element_type=F32)
        kf = jnp.dot(kvn, wkv_ref[:, 0:nw], preferred_element_type=F32)
        vf_t = lax.dot_general(wkv_ref[:, nw:], kvn, tn, preferred_element_type=F32)
        cs_t = cs.T
        for hd in range(MLA_HEADS):
            q_ref[0, hd, 0:QK_NOPE, rows] = (qf_t[hd * QK_NOPE:(hd + 1) * QK_NOPE, :] * scale).astype(BF16)
            pr = qf_t[nw + hd * 2 * QK_ROPE: nw + (hd + 1) * 2 * QK_ROPE, :] * cs_t
            q_ref[0, hd, QK_NOPE:QK_DIM, rows] = ((pr[:QK_ROPE, :] + pr[QK_ROPE:, :]) * scale).astype(BF16)
            k_ref[0, hd, rows, 0:QK_NOPE] = kf[:, hd * QK_NOPE:(hd + 1) * QK_NOPE].astype(BF16)
            k_ref[0, hd, rows, QK_NOPE:QK_DIM] = k_pe
            v_ref[0, hd, 0, 0:V_HEAD, rows] = vf_t[hd * V_HEAD:(hd + 1) * V_HEAD, :].astype(BF16)
            v_ref[0, hd, 0, V_HEAD:, rows] = jnp.ones((V_ONES, slab), BF16)


def _mix_in(x, cs, gain, w_in, q_gain, w_q, kv_gain, w_kv):
    B, S, D = x.shape
    tm = min(TOK_TILE, S)
    ns = S // tm
    T = B * S
    ncol = w_in.shape[1]
    tok = lambda b, i: (b * ns + i, 0)
    return pl.pallas_call(
        _mix_in_body,
        grid=(B, ns),
        in_specs=[
            pl.BlockSpec((1, tm, D), lambda b, i: (b, i, 0)),
            pl.BlockSpec((tm, LANES), tok),
            _resident((1, D)),
            _resident((D, ncol)),
            _resident((1, Q_RANK)),
            _resident(w_q.shape),
            _resident((1, KV_RANK)),
            _resident(w_kv.shape),
        ],
        out_specs=[
            pl.BlockSpec((1, MLA_HEADS, QK_DIM, tm), lambda b, i: (b, 0, 0, i)),
            pl.BlockSpec((1, MLA_HEADS, tm, QK_DIM), lambda b, i: (b, 0, i, 0)),
            pl.BlockSpec((1, MLA_HEADS, 1, V_HEAD + V_ONES, tm), lambda b, i: (b, 0, i, 0, 0)),
            pl.BlockSpec((tm, HG_FDIM + 2 * HG_WIDTH), tok),
            pl.BlockSpec((tm, HG_FDIM), tok),
        ],
        out_shape=[
            jax.ShapeDtypeStruct((B, MLA_HEADS, QK_DIM, S), BF16),
            jax.ShapeDtypeStruct((B, MLA_HEADS, S, QK_DIM), BF16),
            jax.ShapeDtypeStruct((B, MLA_HEADS, ns, V_HEAD + V_ONES, tm), BF16),
            jax.ShapeDtypeStruct((T, HG_FDIM + 2 * HG_WIDTH), BF16),
            jax.ShapeDtypeStruct((T, HG_FDIM), F32),
        ],
        compiler_params=_cparams(("parallel", "parallel")),
        name="mix_in",
    )(x, cs, gain, w_in, q_gain, w_q, kv_gain, w_kv)


def _attn_body(q_ref, k_ref, v_ref, o_ref, s0_ref, s1_ref, m_ref, acc_ref, *, tq, tk, nh):
    qi = pl.program_id(2)

    heads = range(nh)

    def scores(kb, s_ref):
        start = pl.multiple_of(kb * tk, tk)
        for hd in heads:
            s_ref[hd] = jnp.dot(k_ref[0, hd, pl.ds(start, tk), :], q_ref[0, hd],
                                preferred_element_type=F32)

    def update(kb, s_of_head):
        for hd in heads:
            s = s_of_head(hd)
            m = m_ref[hd]
            m_new = jnp.maximum(m, jnp.max(s, axis=0, keepdims=True))
            alpha = jnp.exp2(m - m_new)
            p = jnp.exp2((s - m_new).astype(BF16))
            m_ref[hd] = m_new
            acc_ref[hd] = alpha * acc_ref[hd] + jnp.dot(v_ref[0, hd, kb], p, preferred_element_type=F32)

    def masked(s_ref, kb):
        def get(hd):
            s = s_ref[hd]
            key = lax.broadcasted_iota(jnp.int32, s.shape, 0) + (kb * tk - qi * tq)
            qry = lax.broadcasted_iota(jnp.int32, s.shape, 1)
            return jnp.where(key <= qry, s, NEG_BIG)
        return get

    m_ref[...] = jnp.full(m_ref.shape, NEG_BIG, F32)
    acc_ref[...] = jnp.zeros(acc_ref.shape, F32)
    nfull = qi * (tq // tk)
    scores(0, s0_ref)

    def pair(j, carry):
        scores(2 * j + 1, s1_ref)
        update(2 * j, lambda hd: s0_ref[hd])
        scores(2 * j + 2, s0_ref)
        update(2 * j + 1, lambda hd: s1_ref[hd])
        return carry

    lax.fori_loop(0, nfull // 2, pair, 0)
    scores(nfull + 1, s1_ref)
    update(nfull, masked(s0_ref, nfull))
    update(nfull + 1, masked(s1_ref, nfull + 1))

    for hd in heads:
        out = acc_ref[hd, 0:V_HEAD, :] / acc_ref[hd, V_HEAD:V_HEAD + 1, :]
        o_ref[0, :, hd * V_HEAD:(hd + 1) * V_HEAD] = out.T.astype(o_ref.dtype)


def _attention(q, k, v):
    B, H, S, _ = k.shape
    tk = v.shape[-1]
    tq = 2 * tk
    nh = ATTN_HEADS_PER_STEP
    return pl.pallas_call(
        functools.partial(_attn_body, tq=tq, tk=tk, nh=nh),
        grid=(B, H // nh, S // tq),
        in_specs=[
            pl.BlockSpec((1, nh, QK_DIM, tq), lambda b, h, i: (b, h, 0, i)),
            pl.BlockSpec((1, nh, S, QK_DIM), lambda b, h, i: (b, h, 0, 0)),
            pl.BlockSpec((1, nh, S // tk, V_HEAD + V_ONES, tk), lambda b, h, i: (b, h, 0, 0, 0)),
        ],
        out_specs=pl.BlockSpec((1, tq, nh * V_HEAD), lambda b, h, i: (b, i, h)),
        out_shape=jax.ShapeDtypeStruct((B, S, H * V_HEAD), BF16),
        scratch_shapes=[
            pltpu.VMEM((nh, tk, tq), F32),
            pltpu.VMEM((nh, tk, tq), F32),
            pltpu.VMEM((nh, 1, tq), F32),
            pltpu.VMEM((nh, V_HEAD + V_ONES, tq), F32),
        ],
        compiler_params=_cparams(("parallel", "parallel", "arbitrary")),
        name="mla_attention",
    )(q, k, v)


def _hgrn_body(hqig_ref, hf_ref, lb_ref, gain_ref, r_ref, state_ref, *, n_chunks):
    C = HG_CHUNK

    @pl.when(pl.program_id(1) == 0)
    def _():
        state_ref[...] = jnp.zeros_like(state_ref)

    row = lax.broadcasted_iota(jnp.int32, (C, C), 0)
    col = lax.broadcasted_iota(jnp.int32, (C, C), 1)
    tri = jnp.where(col <= row, 1.0, 0.0).astype(BF16)
    lb = lb_ref[...]
    gain = gain_ref[...]

    levels = []
    L = C // 2
    while L >= HG_SUB:
        levels.append(L)
        L //= 2
    level_masks = [((row // L) % 2 == 1) & ((col // L) % 2 == 0) & (row // (2 * L) == col // (2 * L))
                   for L in levels]
    frow = lax.broadcasted_iota(jnp.int32, (C, HG_FDIM), 0)
    later_half = [(frow // L) % 2 == 1 for L in levels]
    diag_mask = (col // HG_SUB == row // HG_SUB) & (col % HG_SUB <= row % HG_SUB)
    src = lax.broadcasted_iota(jnp.int32, (HG_SUB * HG_KEY, C), 0) // HG_KEY
    dst = lax.broadcasted_iota(jnp.int32, (HG_SUB * HG_KEY, C), 1) % HG_SUB
    spread = jnp.where(src == dst, 1.0, 0.0).astype(BF16)

    def chunk(c, carry):
        r0 = pl.multiple_of(c * C, C)
        rows = pl.ds(r0, C)
        q_raw = hqig_ref[rows, 0:HG_FDIM].astype(F32)
        z = hf_ref[rows, :]
        q = q_raw * _sigmoid(q_raw)
        sig = _sigmoid(z)
        f = lb + (1.0 - lb) * sig
        log_f = jnp.log2(jnp.maximum(f, F_FLOOR))
        k = (1.0 - lb) * (1.0 - sig)

        p0 = log_f.astype(BF16)
        e0 = log_f - p0.astype(F32)
        p1 = e0.astype(BF16)
        p2 = (e0 - p1.astype(F32)).astype(BF16)
        b = (jnp.dot(tri, p0, preferred_element_type=F32)
             + jnp.dot(tri, p1, preferred_element_type=F32)
             + jnp.dot(tri, p2, preferred_element_type=F32))
        b_last = b[C - 1:C, :]
        q_dec = (q * jnp.exp2(b)).astype(BF16)
        k_dec = (k * jnp.exp2(b_last - b)).astype(BF16)
        s_dec = jnp.exp2(b_last)

        nsub = C // HG_SUB
        b3 = b.reshape(nsub, HG_SUB, HG_FDIM)
        q3 = q.reshape(nsub, HG_SUB, HG_FDIM)
        k3 = k.reshape(nsub, HG_SUB, HG_FDIM)
        diag = []
        for s in range(HG_SUB):
            e = jnp.exp2(jnp.minimum(b3 - b3[:, s:s + 1, :], 0.0))
            diag.append((q3 * e * k3[:, s:s + 1, :]).reshape(C, HG_FDIM).astype(BF16))

        lvl = []
        for L, later in zip(levels, later_half):
            g = C // (2 * L)
            bg = b.reshape(g, 2 * L, HG_FDIM)
            ref = jnp.broadcast_to(bg[:, L - 1:L, :], bg.shape).reshape(C, HG_FDIM)
            lvl.append((jnp.where(later, q, k) * jnp.exp2(-jnp.abs(b - ref))).astype(BF16))

        for hd in range(HG_HEADS):
            lo = hd * HG_KEY
            hs = slice(lo, lo + HG_KEY)
            dp = jnp.concatenate([p[:, hs] for p in diag], axis=1)
            a = jnp.where(diag_mask, jnp.dot(dp, spread, preferred_element_type=F32), 0.0)
            for i in range(len(levels)):
                mh = lvl[i][:, hs]
                sc = lax.dot_general(mh, mh, (((1,), (1,)), ((), ())), preferred_element_type=F32)
                a = a + jnp.where(level_masks[i], sc, 0.0)
            v = hqig_ref[rows, HG_FDIM + hd * HG_VAL: HG_FDIM + (hd + 1) * HG_VAL]
            st = state_ref[hd]
            o = jnp.dot(a.astype(BF16), v, preferred_element_type=F32)
            o = o + lax.dot_general(q_dec[:, hs], st.astype(BF16), (((1,), (1,)), ((), ())),
                                    preferred_element_type=F32)
            state_ref[hd] = s_dec[:, hs] * st + lax.dot_general(
                v, k_dec[:, hs], (((0,), (0,)), ((), ())), preferred_element_type=F32)
            g_raw = hqig_ref[rows, HG_FDIM + HG_WIDTH + hd * HG_VAL:
                             HG_FDIM + HG_WIDTH + (hd + 1) * HG_VAL].astype(F32)
            y = _rms(o, gain) * (g_raw * _sigmoid(g_raw))
            r_ref[rows, hd * HG_VAL:(hd + 1) * HG_VAL] = y.astype(r_ref.dtype)
        return carry

    lax.fori_loop(0, n_chunks, chunk, 0, unroll=4)


def _hgrn(hqig, hf, lower, out_gain, B, S):
    tb = min(HG_BLOCK, S)
    nb = S // tb
    T = B * S
    tok = lambda b, i: (b * nb + i, 0)
    return pl.pallas_call(
        functools.partial(_hgrn_body, n_chunks=tb // HG_CHUNK),
        grid=(B, nb),
        in_specs=[
            pl.BlockSpec((tb, HG_FDIM + 2 * HG_WIDTH), tok),
            pl.BlockSpec((tb, HG_FDIM), tok),
            _resident((1, HG_FDIM)),
            _resident((1, HG_VAL)),
        ],
        out_specs=pl.BlockSpec((tb, HG_WIDTH), tok),
        out_shape=jax.ShapeDtypeStruct((T, HG_WIDTH), BF16),
        scratch_shapes=[pltpu.VMEM((HG_HEADS, HG_VAL, HG_KEY), F32)],
        compiler_params=_cparams(("parallel", "arbitrary")),
        name="hgrn2_scan",
    )(hqig, hf, lower, out_gain)


def _mix_out(x_ref, a_ref, r_ref, wo_ref, rows=slice(None)):
    y = jnp.dot(a_ref[rows, :], wo_ref[0:MLA_WIDTH, :], preferred_element_type=F32)
    y = y + jnp.dot(r_ref[rows, :], wo_ref[MLA_WIDTH:, :], preferred_element_type=F32)
    return x_ref[rows, :] + y


def _dense_body(x_ref, a_ref, r_ref, wo_ref, g_ref, wg_ref, wu_ref, wd_ref, *rest, n_split, final):
    if final:
        fg_ref, o_ref = rest
    else:
        (o_ref,) = rest
    x1 = _mix_out(x_ref, a_ref, r_ref, wo_ref)
    h = _rms(x1, g_ref[...]).astype(BF16)
    ff = wg_ref.shape[1]
    step = ff // n_split
    y = x1
    for j in range(n_split):
        cs = slice(j * step, (j + 1) * step)
        gate = jnp.dot(h, wg_ref[:, cs], preferred_element_type=F32)
        up = jnp.dot(h, wu_ref[:, cs], preferred_element_type=F32)
        act = (gate * _sigmoid(gate) * up).astype(BF16)
        y = y + jnp.dot(act, wd_ref[cs, :], preferred_element_type=F32)
    if final:
        y = _rms(y, fg_ref[...])
    o_ref[...] = y


def _dense_layer(x2, a, r, w_out, gain, wg, wu, wd, final_gain):
    T, D = x2.shape
    tm = min(TOK_TILE, T)
    ff = wg.shape[1]
    tok = lambda i: (i, 0)
    final = final_gain is not None
    in_specs = [
        pl.BlockSpec((tm, D), tok),
        pl.BlockSpec((tm, MLA_WIDTH), tok),
        pl.BlockSpec((tm, HG_WIDTH), tok),
        _resident(w_out.shape),
        _resident((1, D)),
        _resident((D, ff)),
        _resident((D, ff)),
        _resident((ff, D)),
    ]
    args = [x2, a, r, w_out, gain, wg, wu, wd]
    if final:
        in_specs.append(_resident((1, D)))
        args.append(final_gain)
    return pl.pallas_call(
        functools.partial(_dense_body, n_split=2, final=final),
        grid=(T // tm,),
        in_specs=in_specs,
        out_specs=pl.BlockSpec((tm, D), tok),
        out_shape=jax.ShapeDtypeStruct((T, D), F32),
        compiler_params=_cparams(("parallel",)),
        name="mix_out_dense_ffn",
    )(*args)


def _router_body(x_ref, a_ref, r_ref, wo_ref, g_ref, wr_hi_ref, wr_lo_ref, x1_ref, h_ref, route_ref):
    tm = x_ref.shape[0]
    slab = tm // ROW_SLABS
    for s in range(ROW_SLABS):
        rows = slice(s * slab, (s + 1) * slab)
        x1 = _mix_out(x_ref, a_ref, r_ref, wo_ref, rows)
        x1_ref[rows, :] = x1
        h = _rms(x1, g_ref[...])
        h_hi = h.astype(BF16)
        h_ref[rows, :] = h_hi
        h_lo = (h - h_hi.astype(F32)).astype(BF16)
        logits = (jnp.dot(h_hi, wr_hi_ref[...], preferred_element_type=F32)
                  + jnp.dot(h_hi, wr_lo_ref[...], preferred_element_type=F32)
                  + jnp.dot(h_lo, wr_hi_ref[...], preferred_element_type=F32))
        lg = logits.T[0:N_EXPERTS, :]
        idx = lax.broadcasted_iota(jnp.int32, lg.shape, 0)
        neg = -jnp.inf
        m1 = jnp.max(lg, axis=0, keepdims=True)
        i1 = jnp.min(jnp.where(lg == m1, idx, N_EXPERTS), axis=0, keepdims=True)
        lg2 = jnp.where(idx == i1, neg, lg)
        m2 = jnp.max(lg2, axis=0, keepdims=True)
        i2 = jnp.min(jnp.where(lg2 == m2, idx, N_EXPERTS), axis=0, keepdims=True)
        e2 = jnp.exp(m2 - m1)
        g1 = 1.0 / (1.0 + e2)
        g2 = e2 / (1.0 + e2)
        route_ref[:, rows] = jnp.concatenate(
            [i1.astype(F32), i2.astype(F32), g1, g2, jnp.zeros((SUBLANES - 2 * TOP_K, slab), F32)], axis=0)


def _router(x2, a, r, w_out, gain, w_router):
    T, D = x2.shape
    tm = min(TOK_TILE, T)
    tok = lambda i: (i, 0)
    return pl.pallas_call(
        _router_body,
        grid=(T // tm,),
        in_specs=[
            pl.BlockSpec((tm, D), tok),
            pl.BlockSpec((tm, MLA_WIDTH), tok),
            pl.BlockSpec((tm, HG_WIDTH), tok),
            _resident(w_out.shape),
            _resident((1, D)),
            _resident((D, LANES)),
            _resident((D, LANES)),
        ],
        out_specs=[pl.BlockSpec((tm, D), tok), pl.BlockSpec((tm, D), tok),
                   pl.BlockSpec((SUBLANES, tm), lambda i: (0, i))],
        out_shape=[jax.ShapeDtypeStruct((T, D), F32), jax.ShapeDtypeStruct((T, D), BF16),
                   jax.ShapeDtypeStruct((SUBLANES, T), F32)],
        compiler_params=_cparams(("parallel",)),
        name="mix_out_router",
    )(x2, a, r, w_out, gain, w_router.astype(BF16),
      (w_router - w_router.astype(BF16).astype(F32)).astype(BF16))


def _group_copies(count, src_ref, src_off, dst_ref, dst_off, sem, act):
    done = 0
    for p in GROUP_CHUNKS:
        take = (count & p) != 0

        @pl.when(take)
        def _():
            src = src_ref.at[pl.ds(pl.multiple_of(src_off + done, GROUP_ALIGN), p), :]
            dst = dst_ref.at[pl.ds(pl.multiple_of(dst_off + done, GROUP_ALIGN), p), :]
            act(pltpu.make_async_copy(src, dst, sem))

        done = done + jnp.where(take, p, 0)


def _dispatch_body(cnt_ref, loc_ref, glb_ref, fill_ref, info_ref, h_ref, xs_ref,
                   sorted_ref, zero_ref, sem, fill_sem, *, tile):
    i = pl.program_id(0)
    n = pl.num_programs(0)
    slot = i % 2

    @pl.when(i == 0)
    def _():
        zero_ref[...] = jnp.zeros_like(zero_ref)
        for e in range(fill_ref.shape[0]):
            @pl.when(fill_ref[e] >= 0)
            def _():
                cp = pltpu.make_async_copy(
                    zero_ref, xs_ref.at[pl.ds(pl.multiple_of(fill_ref[e], tile), tile), :], fill_sem)
                cp.start()
                cp.wait()

    info = info_ref[...]
    rows = sorted_ref.shape[1]
    lane = lax.broadcasted_iota(jnp.int32, (info.shape[0], rows), 1).astype(F32)
    onehot = jnp.where((lane == info[:, 0:1]) | (lane == info[:, 1:2]), 1.0, 0.0).astype(BF16)
    sorted_ref[slot] = lax.dot_general(onehot, h_ref[...], (((0,), (0,)), ((), ())),
                                       preferred_element_type=F32)

    def copies(step, buf, act):
        for e in range(N_EXPERTS):
            j = step * N_EXPERTS + e
            _group_copies(cnt_ref[j], sorted_ref.at[buf], loc_ref[j], xs_ref, glb_ref[j], sem.at[buf], act)

    copies(i, slot, lambda cp: cp.start())

    @pl.when(i > 0)
    def _():
        copies(i - 1, 1 - slot, lambda cp: cp.wait())

    @pl.when(i == n - 1)
    def _():
        copies(i, slot, lambda cp: cp.wait())


def _dispatch(cnt, loc, glb, fill_start, info, h, n_rows, tile, local_rows):
    T, D = h.shape
    tm = min(TOK_TILE, T)
    tok = lambda i, *_: (i, 0)
    grid_spec = pltpu.PrefetchScalarGridSpec(
        num_scalar_prefetch=4,
        grid=(T // tm,),
        in_specs=[pl.BlockSpec((tm, LANES), tok), pl.BlockSpec((tm, D), tok)],
        out_specs=pl.BlockSpec(memory_space=pl.ANY),
        scratch_shapes=[pltpu.VMEM((2, local_rows, D), F32), pltpu.VMEM((tile, D), F32),
                        pltpu.SemaphoreType.DMA((2,)), pltpu.SemaphoreType.DMA(())],
    )
    return pl.pallas_call(
        functools.partial(_dispatch_body, tile=tile),
        grid_spec=grid_spec,
        out_shape=jax.ShapeDtypeStruct((n_rows, D), F32),
        compiler_params=_cparams(("arbitrary",)),
        name="moe_dispatch",
    )(cnt, loc, glb, fill_start, info, h)


def _expert_body(te_ref, nv_ref, xs_ref, wg_ref, wu_ref, wd_ref, ys_ref, acc_ref):
    t = pl.program_id(0)
    j = pl.program_id(1)

    @pl.when(t < nv_ref[0])
    def _():
        x = xs_ref[...].astype(BF16)
        gate = jnp.dot(x, wg_ref[0], preferred_element_type=F32)
        up = jnp.dot(x, wu_ref[0], preferred_element_type=F32)
        act = (gate * _sigmoid(gate) * up).astype(BF16)
        y = jnp.dot(act, wd_ref[0], preferred_element_type=F32)

        @pl.when(j == 0)
        def _():
            acc_ref[...] = y

        @pl.when(j > 0)
        def _():
            acc_ref[...] += y

        @pl.when(j == pl.num_programs(1) - 1)
        def _():
            ys_ref[...] = acc_ref[...]

    @pl.when(t >= nv_ref[0])
    def _():
        ys_ref[...] = jnp.zeros_like(ys_ref)


def _experts(tile_expert, n_valid, xs, wg, wu, wd, tile):
    P, D = xs.shape
    ff = wg.shape[2]
    fs = ff // MOE_FF_SPLIT
    n_tiles = P // tile

    def row_map(t, j, te, nv):
        return (jnp.minimum(t, nv[0] - 1), 0)

    grid_spec = pltpu.PrefetchScalarGridSpec(
        num_scalar_prefetch=2,
        grid=(n_tiles, MOE_FF_SPLIT),
        in_specs=[
            pl.BlockSpec((tile, D), row_map),
            pl.BlockSpec((1, D, fs), lambda t, j, te, nv: (te[t], 0, j)),
            pl.BlockSpec((1, D, fs), lambda t, j, te, nv: (te[t], 0, j)),
            pl.BlockSpec((1, fs, D), lambda t, j, te, nv: (te[t], j, 0)),
        ],
        out_specs=pl.BlockSpec((tile, D), lambda t, j, te, nv: (t, 0)),
        scratch_shapes=[pltpu.VMEM((tile, D), F32)],
    )
    return pl.pallas_call(
        _expert_body,
        grid_spec=grid_spec,
        out_shape=jax.ShapeDtypeStruct((P, D), F32),
        compiler_params=_cparams(("arbitrary", "arbitrary")),
        name="moe_experts",
    )(tile_expert, n_valid, xs, wg, wu, wd)


def _combine_body(cnt_ref, loc_ref, glb_ref, x1_ref, info_ref, ys_ref, *rest, final):
    if final:
        fg_ref, o_ref, local_ref, sem = rest
    else:
        o_ref, local_ref, sem = rest
    i = pl.program_id(0)
    n = pl.num_programs(0)
    slot = i % 2

    def copies(step, buf, act):
        for e in range(N_EXPERTS):
            j = step * N_EXPERTS + e
            _group_copies(cnt_ref[j], ys_ref, glb_ref[j], local_ref.at[buf], loc_ref[j], sem.at[buf], act)

    @pl.when(i == 0)
    def _():
        local_ref[...] = jnp.zeros_like(local_ref)
        copies(0, 0, lambda cp: cp.start())

    @pl.when(i + 1 < n)
    def _():
        copies(i + 1, 1 - slot, lambda cp: cp.start())

    copies(i, slot, lambda cp: cp.wait())

    info = info_ref[...]
    rows = local_ref.shape[1]
    lane = lax.broadcasted_iota(jnp.int32, (info.shape[0], rows), 1).astype(F32)
    weights = (jnp.where(lane == info[:, 0:1], info[:, 2:3], 0.0)
               + jnp.where(lane == info[:, 1:2], info[:, 3:4], 0.0))
    w_hi = weights.astype(BF16)
    w_lo = (weights - w_hi.astype(F32)).astype(BF16)
    ys = local_ref[slot].astype(BF16)
    y = (x1_ref[...] + jnp.dot(w_hi, ys, preferred_element_type=F32)
         + jnp.dot(w_lo, ys, preferred_element_type=F32))
    if final:
        y = _rms(y, fg_ref[...])
    o_ref[...] = y


def _combine(cnt, loc, glb, x1, info, ys, final_gain, local_rows):
    T, D = x1.shape
    tm = min(TOK_TILE, T)
    final = final_gain is not None
    tok = lambda i, *_: (i, 0)
    in_specs = [
        pl.BlockSpec((tm, D), tok),
        pl.BlockSpec((tm, LANES), tok),
        pl.BlockSpec(memory_space=pl.ANY),
    ]
    args = [cnt, loc, glb, x1, info, ys]
    if final:
        in_specs.append(pl.BlockSpec((1, D), lambda i, *_: (0, 0)))
        args.append(final_gain)
    grid_spec = pltpu.PrefetchScalarGridSpec(
        num_scalar_prefetch=3,
        grid=(T // tm,),
        in_specs=in_specs,
        out_specs=pl.BlockSpec((tm, D), tok),
        scratch_shapes=[pltpu.VMEM((2, local_rows, D), F32), pltpu.SemaphoreType.DMA((2,))],
    )
    return pl.pallas_call(
        functools.partial(_combine_body, final=final),
        grid_spec=grid_spec,
        out_shape=jax.ShapeDtypeStruct((T, D), F32),
        compiler_params=_cparams(("arbitrary",)),
        name="moe_combine",
    )(*args)


def _moe_layer(x2, a, r, w_out, gain, w_router, wg, wu, wd, final_gain):
    T, D = x2.shape
    tile = min(MOE_TM, T)
    tm = min(TOK_TILE, T)
    nt = T // tm
    wr = jnp.zeros((D, LANES), F32).at[:, :N_EXPERTS].set(w_router)
    x1, h, route = _router(x2, a, r, w_out, gain, wr)

    experts = jnp.arange(N_EXPERTS, dtype=jnp.int32)
    e = route[:TOP_K].astype(jnp.int32).reshape(TOP_K, nt, tm)
    oh0 = (e[0, :, :, None] == experts).astype(jnp.int32)
    oh1 = (e[1, :, :, None] == experts).astype(jnp.int32)
    earlier = jnp.tril(jnp.ones((tm, tm), F32), -1)
    before0 = jnp.einsum('ts,nse->nte', earlier, oh0.astype(F32)).astype(jnp.int32)
    before1 = jnp.einsum('ts,nse->nte', earlier, oh1.astype(F32)).astype(jnp.int32)
    cnt0 = jnp.sum(oh0, axis=1)
    cnt = ((cnt0 + jnp.sum(oh1, axis=1) + GROUP_ALIGN - 1) // GROUP_ALIGN) * GROUP_ALIGN
    loc = jnp.cumsum(cnt, axis=1) - cnt
    lpos0 = jnp.sum(oh0 * (loc[:, None, :] + before0), axis=-1)
    lpos1 = jnp.sum(oh1 * (loc[:, None, :] + cnt0[:, None, :] + before1), axis=-1)
    total = jnp.sum(cnt, axis=0)
    padded = ((total + tile - 1) // tile) * tile
    ends = jnp.cumsum(padded)
    glb = (ends - padded)[None, :] + jnp.cumsum(cnt, axis=0) - cnt

    local_rows = -(-(tm * TOP_K + N_EXPERTS * (GROUP_ALIGN - 1)) // LANES) * LANES
    n_rows = -(-(T * TOP_K + nt * N_EXPERTS * (GROUP_ALIGN - 1)) // tile) * tile + N_EXPERTS * tile
    tile_row = jnp.arange(n_rows // tile, dtype=jnp.int32) * tile
    tail_row = tile_row[(T * TOP_K) // tile:]
    fill_start = jnp.concatenate([jnp.where(padded > 0, ends - tile, -1),
                                  jnp.where(tail_row >= ends[-1], tail_row, -1)]).astype(jnp.int32)
    tile_expert = jnp.minimum(jnp.sum(tile_row[:, None] >= ends[None, :], axis=1), N_EXPERTS - 1).astype(jnp.int32)
    n_valid = (ends[-1:] // tile).astype(jnp.int32)
    info = jnp.concatenate([lpos0.reshape(T, 1).astype(F32), lpos1.reshape(T, 1).astype(F32),
                            route[TOP_K:2 * TOP_K].T, jnp.zeros((T, LANES - 4), F32)], axis=1)
    cnt, loc, glb = (t.reshape(-1).astype(jnp.int32) for t in (cnt, loc, glb))

    xs = _dispatch(cnt, loc, glb, fill_start, info, h, n_rows, tile, local_rows)
    ys = _experts(tile_expert, n_valid, xs, wg, wu, wd, tile)
    return _combine(cnt, loc, glb, x1, info, ys, final_gain, local_rows)


def _rotate_half_cols(w):
    half = QK_ROPE // 2
    return jnp.concatenate([-w[..., half:], w[..., :half]], axis=-1)


def _prep_w_in(w):
    o = Q_RANK + KV_RANK
    k_pe = w[:, o:o + QK_ROPE]
    return jnp.concatenate([w[:, :o], w[:, o + QK_ROPE:], k_pe, _rotate_half_cols(k_pe)], axis=1).astype(BF16)


def _prep_w_q(w):
    w3 = w.reshape(Q_RANK, MLA_HEADS, QK_DIM)
    nope = w3[:, :, :QK_NOPE].reshape(Q_RANK, MLA_HEADS * QK_NOPE)
    pe = w3[:, :, QK_NOPE:]
    pair = jnp.concatenate([pe, _rotate_half_cols(pe)], axis=-1).reshape(Q_RANK, MLA_HEADS * 2 * QK_ROPE)
    return jnp.concatenate([nope, pair], axis=1).astype(BF16)


def _prep_w_kv(w):
    w4 = w.reshape(KV_RANK, MLA_HEADS, 2, QK_NOPE)
    return w4.transpose(0, 2, 1, 3).reshape(KV_RANK, 2 * MLA_HEADS * QK_NOPE).astype(BF16)


def kernel(x, positions, mix_norm, w_in, q_a_norm, w_q_b, kv_a_norm, w_kv_b, hg_lower_bounds, hg_out_norm, w_out, ffn_norm, dense_w_gate, dense_w_up, dense_w_down, moe_router, moe_w_gate, moe_w_up, moe_w_down, final_norm):
    B, S, D = x.shape
    depth = w_in.shape[0]
    lb_p = jax.nn.softmax(hg_lower_bounds.astype(F32), axis=0)
    lower = jnp.cumsum(lb_p, axis=0) - lb_p[0:1]
    inv = ROPE_THETA ** (-jnp.arange(0, QK_ROPE, 2, dtype=F32) / QK_ROPE)
    invf = jnp.tile(inv, 2 * QK_ROPE // inv.shape[0]).reshape(1, LANES)
    cs = _rope_table(positions.reshape(B * S, 1), invf)
    final_gain = final_norm.reshape(1, D)

    x2 = x.reshape(B * S, D)
    for l in range(depth):
        q, k, v, hqig, hf = _mix_in(
            x2.reshape(B, S, D), cs, mix_norm[l].reshape(1, D), _prep_w_in(w_in[l]),
            q_a_norm[l].reshape(1, Q_RANK), _prep_w_q(w_q_b[l]),
            kv_a_norm[l].reshape(1, KV_RANK), _prep_w_kv(w_kv_b[l]))
        a = _attention(q, k, v).reshape(B * S, MLA_WIDTH)
        r = _hgrn(hqig, hf, lower[l].reshape(1, HG_FDIM), hg_out_norm[l].reshape(1, HG_VAL), B, S)
        fg = final_gain if l == depth - 1 else None
        wo = w_out[l].astype(BF16)
        gain = ffn_norm[l].reshape(1, D)
        j = l // 2
        if l % 2 == 0:
            x2 = _dense_layer(x2, a, r, wo, gain, dense_w_gate[j].astype(BF16), dense_w_up[j].astype(BF16),
                              dense_w_down[j].astype(BF16), fg)
        else:
            x2 = _moe_layer(x2, a, r, wo, gain, moe_router[j], moe_w_gate[j].astype(BF16),
                            moe_w_up[j].astype(BF16), moe_w_down[j].astype(BF16), fg)
    return x2.reshape(B, S, D)
```
